```python
import math
import jax
import jax.numpy as jnp
from jax import lax
import numpy as np

D_MODEL = 1024
BATCH = 8
SEQ = 2048
DEPTH = 4
DEC_BATCH = 128
DEC_SEQ = 8
PAST_LEN = 16384
PAGE_SIZE = 128

M_EXPAND = 2
M_INNER = M_EXPAND * D_MODEL
M_HEADDIM = 64
M_HEADS = M_INNER // M_HEADDIM
M_GROUPS = 8
M_STATE = 128
M_CONV = 4
M_CONV_DIM = M_INNER + 2 * M_GROUPS * M_STATE
M_CHUNK = 128
RMS_EPS = 1e-5
R_WIDTH = D_MODEL
R_HEADSIZE = 64
R_HEADS = R_WIDTH // R_HEADSIZE
R_DECAY_LORA = 64
R_A_LORA = 64
R_GATE_LORA = 128
R_SHIFT_W = 3 * R_WIDTH + R_DECAY_LORA + R_A_LORA + R_GATE_LORA
R_GN_EPS = 64e-5
MEM_LEN = 256
X_HEADS = 4
X_WIDTH = D_MODEL
X_HEADDIM = X_WIDTH // X_HEADS
N_BRANCH = 3
OFF_XBC = M_INNER
OFF_DT = OFF_XBC + M_CONV_DIM
OFF_RWKV = OFF_DT + M_HEADS
OFF_Q = OFF_RWKV + R_SHIFT_W
OFF_GATE = OFF_Q + X_WIDTH
IN_W = OFF_GATE + N_BRANCH * D_MODEL
N_EGROUPS = 4
E_PER_GROUP = 8
N_EXPERTS = N_EGROUPS * E_PER_GROUP
E_TOPK = 2
D_EXPERT = 512
MOE_BLOCK = 128
LN_EPS = 1e-5
ALPHA = (2 * DEPTH) ** 0.25
BETA = (8 * DEPTH) ** -0.25

kernel_name = 'hybrid_ssd_rwkv7_memattn_hiermoe_step'


def layer_norm(x, g, b):
    xf = x.astype(jnp.float32)
    mu = jnp.mean(xf, -1, keepdims=True)
    var = jnp.mean(jnp.square(xf - mu), -1, keepdims=True)
    y = (xf - mu) * lax.rsqrt(var + LN_EPS) * g.astype(jnp.float32) + b.astype(jnp.float32)
    return y.astype(x.dtype)


def causal_conv(u, buf, w, bias):
    L = u.shape[1]
    full = jnp.concatenate([buf.astype(u.dtype), u], axis=1)
    out = bias + sum(full[:, j:j + L] * w[j] for j in range(M_CONV))
    return out, full[:, L:]


def ssd_scan(xdt, adt, bmat, cmat, h0):
    f32 = jnp.float32
    bsz, L = xdt.shape[:2]
    q = M_CHUNK if L >= M_CHUNK else L
    pad = (-L) % q
    if pad:
        pw = lambda t: jnp.pad(t, [(0, 0), (0, pad)] + [(0, 0)] * (t.ndim - 2))
        xdt, adt, bmat, cmat = pw(xdt), pw(adt), pw(bmat), pw(cmat)
    nc = (L + pad) // q
    R = M_HEADS // M_GROUPS
    X = xdt.astype(f32).reshape(bsz, nc, q, M_GROUPS, R, M_HEADDIM)
    A = adt.astype(f32).reshape(bsz, nc, q, M_GROUPS, R)
    Bc = bmat.astype(f32).reshape(bsz, nc, q, M_GROUPS, M_STATE)
    Cc = cmat.astype(f32).reshape(bsz, nc, q, M_GROUPS, M_STATE)
    a_cs = jnp.cumsum(A, axis=2)
    causal = jnp.tril(jnp.ones((q, q), bool))[None, None, :, :, None, None]
    seg = a_cs[:, :, :, None] - a_cs[:, :, None, :]
    lmat = jnp.exp(jnp.where(causal, seg, -jnp.inf))
    cb = jnp.einsum('bclgn,bcsgn->bclsg', Cc, Bc)
    y_diag = jnp.einsum('bclsgr,bcsgrp->bclgrp', cb[..., None] * lmat, X)
    s_chunk = jnp.einsum('bclgn,bclgrp->bcgrpn', Bc, X * jnp.exp(a_cs[:, :, -1:] - a_cs)[..., None])
    d_chunk = jnp.exp(a_cs[:, :, -1])

    def step(h, inp):
        s_c, d_c = inp
        return h * d_c[..., None, None] + s_c, h

    h_last, h_in = lax.scan(step, h0.astype(f32).reshape(bsz, M_GROUPS, R, M_HEADDIM, M_STATE),
                            (jnp.moveaxis(s_chunk, 1, 0), jnp.moveaxis(d_chunk, 1, 0)))
    y_off = jnp.einsum('bclgn,cbgrpn->bclgrp', Cc, h_in) * jnp.exp(a_cs)[..., None]
    y = (y_diag + y_off).reshape(bsz, nc * q, M_HEADS, M_HEADDIM)[:, :L]
    return y, h_last.reshape(bsz, M_HEADS, M_HEADDIM, M_STATE)


def mamba_branch(z, xbc, dt, conv_buf, h0, lw):
    f32 = jnp.float32
    bsz, L, _ = z.shape
    xbc, conv_new = causal_conv(xbc, conv_buf, lw['conv_w'], lw['conv_b'])
    xbc = jax.nn.silu(xbc)
    xs = xbc[..., :M_INNER].reshape(bsz, L, M_HEADS, M_HEADDIM).astype(f32)
    bm = xbc[..., M_INNER:M_INNER + M_GROUPS * M_STATE].reshape(bsz, L, M_GROUPS, M_STATE)
    cm = xbc[..., M_INNER + M_GROUPS * M_STATE:].reshape(bsz, L, M_GROUPS, M_STATE)
    dt = jax.nn.softplus(dt.astype(f32) + lw['dt_bias'].astype(f32))
    a = -jnp.exp(lw['a_log'].astype(f32))
    y, h_last = ssd_scan(xs * dt[..., None], a * dt, bm, cm, h0)
    y = y + xs * lw['d_skip'].astype(f32)[:, None]
    y = y.reshape(bsz, L, M_INNER) * jax.nn.silu(z.astype(f32))
    yg = y.reshape(bsz, L, M_GROUPS, M_INNER // M_GROUPS)
    yg = yg * lax.rsqrt(jnp.mean(jnp.square(yg), -1, keepdims=True) + RMS_EPS)
    y = yg.reshape(bsz, L, M_INNER) * lw['m_norm_w'].astype(f32)
    return y.astype(z.dtype), conv_new, h_last.astype(z.dtype)


def wkv_recurrence(r, w, k, v, a, b, s0):
    def step(s, inp):
        r_t, w_t, k_t, v_t, a_t, b_t = inp
        sa = jnp.einsum('bhij,bhj->bhi', s, a_t)
        s = s * w_t[:, :, None, :] + sa[..., None] * b_t[:, :, None, :] + v_t[..., None] * k_t[:, :, None, :]
        return s, jnp.einsum('bhij,bhj->bhi', s, r_t)

    tm = lambda t: jnp.moveaxis(t, 1, 0)
    s_last, ys = lax.scan(step, s0, (tm(r), tm(w), tm(k), tm(v), tm(a), tm(b)))
    return jnp.moveaxis(ys, 0, 1), s_last


def rwkv_branch(cols, shift_buf, s0, lw):
    f32 = jnp.float32
    bsz, L, _ = cols.shape
    prev = jnp.concatenate([shift_buf[:, None, :].astype(cols.dtype), cols[:, :-1]], axis=1)
    mixed = cols + (prev - cols) * lw['r_mu']
    c1, c2, c3 = R_WIDTH, 2 * R_WIDTH, 3 * R_WIDTH
    c4, c5 = c3 + R_DECAY_LORA, c3 + R_DECAY_LORA + R_A_LORA
    r, k, v = mixed[..., :c1], mixed[..., c1:c2], mixed[..., c2:c3]
    wl, al, gl = mixed[..., c3:c4], mixed[..., c4:c5], mixed[..., c5:]
    w_log = -jax.nn.softplus(-(lw['r_w0'] + jnp.tanh(wl) @ lw['r_w2']).astype(f32)) - 0.5
    decay = jnp.exp(-jnp.exp(w_log))
    a = jax.nn.sigmoid((lw['r_a0'] + al @ lw['r_a2']).astype(f32))
    g = (jax.nn.sigmoid(gl) @ lw['r_g2']).astype(f32)
    hd = lambda t: t.reshape(bsz, L, R_HEADS, R_HEADSIZE).astype(f32)
    kf = k.astype(f32)
    kk = hd(kf * lw['r_kk'].astype(f32))
    kk = kk / jnp.maximum(jnp.sqrt(jnp.sum(jnp.square(kk), -1, keepdims=True)), 1e-12)
    kf = kf * (1.0 + (a - 1.0) * lw['r_ka'].astype(f32))
    rh, kh, vh, ah = hd(r), hd(kf), hd(v), hd(a)
    y, s_last = wkv_recurrence(rh, hd(decay), kh, vh, -kk, kk * ah, s0.astype(f32))
    mu = jnp.mean(y, -1, keepdims=True)
    var = jnp.mean(jnp.square(y - mu), -1, keepdims=True)
    y = ((y - mu) * lax.rsqrt(var + R_GN_EPS)).reshape(bsz, L, R_WIDTH)
    y = y * lw['r_lnx_w'].astype(f32) + lw['r_lnx_b'].astype(f32)
    bonus = jnp.sum(rh * kh * lw['r_rk'].astype(f32), -1, keepdims=True) * vh
    y = y + bonus.reshape(bsz, L, R_WIDTH)
    return (y * g).astype(cols.dtype), cols[:, -1], s_last.astype(cols.dtype)


def mem_attend(q, mem_k, mem_v):
    s = jnp.einsum('blhd,bmhd->bhlm', q, mem_k).astype(jnp.float32) * (X_HEADDIM ** -0.5)
    p = jax.nn.softmax(s, axis=-1).astype(mem_v.dtype)
    return jnp.einsum('bhlm,bmhd->blhd', p, mem_v)


def hier_moe(x, lw):
    f32 = jnp.float32
    bsz, L, D = x.shape
    h = x.reshape(bsz * L, D)
    T = h.shape[0]
    lg = (h @ lw['w_rg']).astype(f32) + lw['b_rg'].astype(f32)
    g_sel = jnp.argmax(lg, axis=-1).astype(jnp.int32)
    g_prob = jnp.take_along_axis(jax.nn.softmax(lg, axis=-1), g_sel[:, None], axis=-1)
    le = ((h @ lw['w_re']).astype(f32) + lw['b_re'].astype(f32)).reshape(T, N_EGROUPS, E_PER_GROUP)
    le = jnp.take_along_axis(le, g_sel[:, None, None], axis=1)[:, 0]
    top_v, top_i = lax.top_k(le, E_TOPK)
    gate = g_prob * jax.nn.softmax(top_v, axis=-1)
    e_idx = g_sel[:, None] * E_PER_GROUP + top_i.astype(jnp.int32)
    n_assign = T * E_TOPK
    flat_e = e_idx.reshape(-1)
    order = jnp.argsort(flat_e)
    se = flat_e[order]
    stok = (order // E_TOPK).astype(jnp.int32)
    counts = jnp.bincount(flat_e, length=N_EXPERTS).astype(jnp.int32)
    padded = (counts + MOE_BLOCK - 1) // MOE_BLOCK * MOE_BLOCK
    pend = jnp.cumsum(padded)
    pstart = pend - padded
    cstart = jnp.cumsum(counts) - counts
    dest = pstart[se] + jnp.arange(n_assign, dtype=jnp.int32) - cstart[se]
    n_blk = (n_assign + MOE_BLOCK - 1) // MOE_BLOCK + N_EXPERTS
    slot_tok = jnp.full((n_blk * MOE_BLOCK,), T, jnp.int32).at[dest].set(stok)
    blk_exp = jnp.minimum(jnp.searchsorted(pend, jnp.arange(n_blk, dtype=jnp.int32) * MOE_BLOCK, side='right'),
                          N_EXPERTS - 1)
    hp = jnp.concatenate([h, jnp.zeros((1, D), h.dtype)], axis=0)
    xb = hp[slot_tok].reshape(n_blk, MOE_BLOCK, D)

    def expert_block(args):
        xblk, e = args
        hid = jax.nn.silu(xblk @ lw['w_gate'][e]) * (xblk @ lw['w_up'][e])
        return hid @ lw['w_down'][e]

    yb = lax.map(expert_block, (xb, blk_exp)).reshape(n_blk * MOE_BLOCK, D)
    y_assign = yb[dest] * gate.reshape(-1)[order][:, None].astype(yb.dtype)
    out = jax.ops.segment_sum(y_assign, stok, num_segments=T)
    return out.reshape(bsz, L, D)


def trunk_layer(x, lw, mem_k, mem_v, conv_buf, ssm_h, shift_buf, wkv_s):
    bsz, L, _ = x.shape
    p = x @ lw['w_in']
    gates = jax.nn.sigmoid(p[..., OFF_GATE:])
    y_a, conv_new, ssm_new = mamba_branch(p[..., :OFF_XBC], p[..., OFF_XBC:OFF_DT], p[..., OFF_DT:OFF_RWKV],
                                          conv_buf, ssm_h, lw)
    y_b, shift_new, wkv_new = rwkv_branch(p[..., OFF_RWKV:OFF_Q], shift_buf, wkv_s, lw)
    y_c = mem_attend(p[..., OFF_Q:OFF_GATE].reshape(bsz, L, X_HEADS, X_HEADDIM), mem_k, mem_v)
    y_c = y_c.reshape(bsz, L, X_WIDTH)
    merged = (gates[..., :D_MODEL] * (y_a @ lw['w_mo'])
              + gates[..., D_MODEL:2 * D_MODEL] * (y_b @ lw['w_ro'])
              + gates[..., 2 * D_MODEL:] * (y_c @ lw['w_xo']))
    x = layer_norm(ALPHA * x + merged @ lw['w_o'], lw['ln1_g'], lw['ln1_b'])
    x = layer_norm(ALPHA * x + hier_moe(x, lw), lw['ln2_g'], lw['ln2_b'])
    return x, conv_new, ssm_new, shift_new, wkv_new


def setup_inputs(seed: int = 0) -> dict:
    key = jax.random.key(seed)
    keys = iter(jax.random.split(key, 64))

    def nrm(shape, scale):
        return scale * jax.random.normal(next(keys), shape, jnp.float32)

    def uni(shape, lo, hi):
        return jax.random.uniform(next(keys), shape, jnp.float32, lo, hi)

    L = DEPTH
    dt0 = jnp.exp(uni((L, M_HEADS), math.log(1e-3), math.log(1e-1)))
    return {
        'x_prompt': nrm((BATCH, SEQ, D_MODEL), 1.0),
        'x_sample': nrm((DEC_BATCH, DEC_SEQ, D_MODEL), 1.0),
        'mem_prompt': nrm((BATCH, MEM_LEN, D_MODEL), 1.0),
        'state_ssm': nrm((L, DEC_BATCH, M_HEADS, M_HEADDIM, M_STATE), 0.1),
        'state_conv': nrm((L, DEC_BATCH, M_CONV - 1, M_CONV_DIM), 1.0),
        'state_wkv': nrm((L, DEC_BATCH, R_HEADS, R_HEADSIZE, R_HEADSIZE), 0.1),
        'state_shift': nrm((L, DEC_BATCH, R_SHIFT_W), 1.0),
        'cache_mem_k': nrm((L, DEC_BATCH, MEM_LEN, X_HEADS, X_HEADDIM), 1.0),
        'cache_mem_v': nrm((L, DEC_BATCH, MEM_LEN, X_HEADS, X_HEADDIM), 1.0),
        'w_in': nrm((L, D_MODEL, IN_W), D_MODEL ** -0.5),
        'conv_w': nrm((L, M_CONV, M_CONV_DIM), M_CONV ** -0.5),
        'conv_b': nrm((L, M_CONV_DIM), 0.02),
        'dt_bias': dt0 + jnp.log(-jnp.expm1(-dt0)),
        'a_log': jnp.log(uni((L, M_HEADS), 1.0, 16.0)),
        'd_skip': 1.0 + nrm((L, M_HEADS), 0.1),
        'm_norm_w': 1.0 + nrm((L, M_INNER), 0.02),
        'r_mu': uni((L, R_SHIFT_W), 0.0, 1.0),
        'r_w0': uni((L, R_WIDTH), -6.0, 1.0),
        'r_w2': nrm((L, R_DECAY_LORA, R_WIDTH), 0.5 * R_DECAY_LORA ** -0.5),
        'r_a0': nrm((L, R_WIDTH), 0.1),
        'r_a2': nrm((L, R_A_LORA, R_WIDTH), 0.5 * R_A_LORA ** -0.5),
        'r_g2': nrm((L, R_GATE_LORA, R_WIDTH), R_GATE_LORA ** -0.5),
        'r_kk': 0.85 + nrm((L, R_WIDTH), 0.05),
        'r_ka': 1.0 + nrm((L, R_WIDTH), 0.05),
        'r_rk': nrm((L, R_HEADS, R_HEADSIZE), 0.1),
        'r_lnx_w': 1.0 + nrm((L, R_WIDTH), 0.02),
        'r_lnx_b': nrm((L, R_WIDTH), 0.02),
        'w_mem_kv': nrm((L, D_MODEL, 2 * X_WIDTH), D_MODEL ** -0.5),
        'w_mo': nrm((L, M_INNER, D_MODEL), M_INNER ** -0.5),
        'w_ro': nrm((L, R_WIDTH, D_MODEL), R_WIDTH ** -0.5),
        'w_xo': nrm((L, X_WIDTH, D_MODEL), X_WIDTH ** -0.5),
        'w_o': nrm((L, D_MODEL, D_MODEL), BETA * D_MODEL ** -0.5),
        'ln1_g': 1.0 + nrm((L, D_MODEL), 0.02),
        'ln1_b': nrm((L, D_MODEL), 0.02),
        'w_rg': nrm((L, D_MODEL, N_EGROUPS), D_MODEL ** -0.5),
        'b_rg': nrm((L, N_EGROUPS), 0.01),
        'w_re': nrm((L, D_MODEL, N_EXPERTS), D_MODEL ** -0.5),
        'b_re': nrm((L, N_EXPERTS), 0.01),
        'w_gate': nrm((L, N_EXPERTS, D_MODEL, D_EXPERT), D_MODEL ** -0.5),
        'w_up': nrm((L, N_EXPERTS, D_MODEL, D_EXPERT), D_MODEL ** -0.5),
        'w_down': nrm((L, N_EXPERTS, D_EXPERT, D_MODEL), BETA * D_EXPERT ** -0.5),
        'ln2_g': 1.0 + nrm((L, D_MODEL), 0.02),
        'ln2_b': nrm((L, D_MODEL), 0.02),
    }


def reference(x_prompt, x_sample, mem_prompt, state_ssm, state_conv, state_wkv, state_shift,
              cache_mem_k, cache_mem_v, w_in, conv_w, conv_b, dt_bias, a_log, d_skip, m_norm_w,
              r_mu, r_w0, r_w2, r_a0, r_a2, r_g2, r_kk, r_ka, r_rk, r_lnx_w, r_lnx_b,
              w_mem_kv, w_mo, w_ro, w_xo, w_o, ln1_g, ln1_b,
              w_rg, b_rg, w_re, b_re, w_gate, w_up, w_down, ln2_g, ln2_b):
    dtype = x_prompt.dtype
    bp = x_prompt.shape[0]
    n_mem = mem_prompt.shape[1]
    zero_conv = jnp.zeros((bp, M_CONV - 1, M_CONV_DIM), dtype)
    zero_ssm = jnp.zeros((bp, M_HEADS, M_HEADDIM, M_STATE), dtype)
    zero_shift = jnp.zeros((bp, R_SHIFT_W), dtype)
    zero_wkv = jnp.zeros((bp, R_HEADS, R_HEADSIZE, R_HEADSIZE), dtype)
    xp, xs = x_prompt, x_sample
    p_ssm, p_conv, p_wkv, p_shift, p_mk, p_mv = [], [], [], [], [], []
    s_ssm, s_conv, s_wkv, s_shift = [], [], [], []
    for l in range(DEPTH):
        lw = dict(w_in=w_in[l], conv_w=conv_w[l], conv_b=conv_b[l], dt_bias=dt_bias[l], a_log=a_log[l],
                  d_skip=d_skip[l], m_norm_w=m_norm_w[l], r_mu=r_mu[l], r_w0=r_w0[l], r_w2=r_w2[l],
                  r_a0=r_a0[l], r_a2=r_a2[l], r_g2=r_g2[l], r_kk=r_kk[l], r_ka=r_ka[l], r_rk=r_rk[l],
                  r_lnx_w=r_lnx_w[l], r_lnx_b=r_lnx_b[l], w_mo=w_mo[l], w_ro=w_ro[l], w_xo=w_xo[l],
                  w_o=w_o[l], ln1_g=ln1_g[l], ln1_b=ln1_b[l], w_rg=w_rg[l], b_rg=b_rg[l], w_re=w_re[l],
                  b_re=b_re[l], w_gate=w_gate[l], w_up=w_up[l], w_down=w_down[l], ln2_g=ln2_g[l],
                  ln2_b=ln2_b[l])
        mkv = mem_prompt @ w_mem_kv[l]
        mk = mkv[..., :X_WIDTH].reshape(bp, n_mem, X_HEADS, X_HEADDIM)
        mv = mkv[..., X_WIDTH:].reshape(bp, n_mem, X_HEADS, X_HEADDIM)
        xp, cp, hp, shp, wp = trunk_layer(xp, lw, mk, mv, zero_conv, zero_ssm, zero_shift, zero_wkv)
        xs, cs, hs, shs, ws = trunk_layer(xs, lw, cache_mem_k[l], cache_mem_v[l], state_conv[l],
                                          state_ssm[l], state_shift[l], state_wkv[l])
        p_ssm.append(hp)
        p_conv.append(cp)
        p_wkv.append(wp)
        p_shift.append(shp)
        p_mk.append(mk)
        p_mv.append(mv)
        s_ssm.append(hs)
        s_conv.append(cs)
        s_wkv.append(ws)
        s_shift.append(shs)
    return (xp, xs, jnp.stack(p_ssm), jnp.stack(p_conv), jnp.stack(p_wkv), jnp.stack(p_shift),
            jnp.stack(p_mk), jnp.stack(p_mv), jnp.stack(s_ssm), jnp.stack(s_conv), jnp.stack(s_wkv),
            jnp.stack(s_shift))
```

```python
import functools

import jax
import jax.numpy as jnp
from jax import lax
from jax.experimental import pallas as pl
from jax.experimental.pallas import tpu as pltpu

F32 = jnp.float32
BF16 = jnp.bfloat16
HIGHEST = lax.Precision.HIGHEST

LANES = 128
SUBLANES = 8
VMEM_LIMIT = 56 * 1024 * 1024

D = 1024
DEPTH = 4
M_INNER = 2048
M_HEADDIM = 64
M_HEADS = 32
M_GROUPS = 8
M_HPG = M_HEADS // M_GROUPS
M_STATE = 128
M_CONV = 4
M_CONV_DIM = M_INNER + 2 * M_GROUPS * M_STATE
M_CHUNK = 128
RMS_EPS = 1e-5
R_HEADS = 16
R_HS = 64
R_LORA_WA = 128
R_LORA_G = 128
R_SHIFT_W = 3 * D + R_LORA_WA + R_LORA_G
R_GN_EPS = 64e-5
MEM_LEN = 256
X_HEADS = 4
X_HEADDIM = 256
OFF_XBC = M_INNER
OFF_DT = OFF_XBC + M_CONV_DIM
OFF_RWKV = OFF_DT + M_HEADS
OFF_Q = OFF_RWKV + R_SHIFT_W
OFF_GATE = OFF_Q + D
N_EGROUPS = 4
E_PER_GROUP = 8
N_EXPERTS = 32
D_EXPERT = 512
LN_EPS = 1e-5
ALPHA = (2 * DEPTH) ** 0.25


def _params(*sem):
    return pltpu.CompilerParams(dimension_semantics=sem, vmem_limit_bytes=VMEM_LIMIT)


def _sigmoid(x):
    return 1.0 / (1.0 + jnp.exp(-x))


def _silu(x):
    return x * _sigmoid(x)


def _softplus(x):
    return jnp.maximum(x, 0.0) + jnp.log1p(jnp.exp(-jnp.abs(x)))


def _layer_norm(h, g, b):
    mu = jnp.mean(h, axis=-1, keepdims=True)
    d = h - mu
    var = jnp.mean(d * d, axis=-1, keepdims=True)
    return d * lax.rsqrt(var + LN_EPS) * g + b


def _proj_kernel(x_ref, w_ref, o_ref, xb_ref):
    @pl.when(pl.program_id(1) == 0)
    def _():
        xb_ref[...] = x_ref[...].astype(BF16)

    o_ref[...] = jnp.dot(xb_ref[...], w_ref[...], preferred_element_type=F32)


def _proj(x, w, tm, tn):
    m, k = x.shape
    n = w.shape[1]
    tm = min(tm, m)
    tn = min(tn, n)
    return pl.pallas_call(
        _proj_kernel,
        grid=(m // tm, n // tn),
        in_specs=[pl.BlockSpec((tm, k), lambda i, j: (i, 0)),
                  pl.BlockSpec((k, tn), lambda i, j: (0, j))],
        out_specs=pl.BlockSpec((tm, tn), lambda i, j: (i, j)),
        out_shape=jax.ShapeDtypeStruct((m, n), F32),
        scratch_shapes=[pltpu.VMEM((tm, k), BF16)],
        compiler_params=_params("arbitrary", "arbitrary"),
        name="proj",
    )(x, w)


def _mamba_kernel(z_ref, xbc_ref, dt_ref, cbuf_ref, h0_ref, convw_ref, convb_ref, dtb_ref, alog_ref,
                  dskip_ref, mnw_ref, y_ref, hT_ref, ubuf, h_scr, yT_scr, *, rows, nc):
    q = M_CHUNK
    c = pl.program_id(1)

    @pl.when(c == 0)
    def _():
        if rows < q:
            ubuf[...] = jnp.zeros(ubuf.shape, F32)
        ubuf[0:SUBLANES, :] = cbuf_ref[0]
        h_scr[...] = h0_ref[0]

    ubuf[SUBLANES:SUBLANES + rows, :] = xbc_ref[...]
    cw = convw_ref[...]
    acc = ubuf[5:5 + q, :] * cw[0:1]
    acc = acc + ubuf[6:6 + q, :] * cw[1:2]
    acc = acc + ubuf[7:7 + q, :] * cw[2:3]
    acc = acc + ubuf[8:8 + q, :] * cw[3:4]
    acc = acc + convb_ref[...]
    if nc > 1:
        ubuf[0:SUBLANES, :] = ubuf[q:q + SUBLANES, :]
    act = _silu(acc)

    row_id = lax.broadcasted_iota(jnp.int32, (q, q), 0)
    col_id = lax.broadcasted_iota(jnp.int32, (q, q), 1)
    dt_raw = dt_ref[...]
    if rows < q:
        dt_raw = jnp.concatenate([dt_raw, jnp.zeros((q - rows, LANES), F32)], axis=0)
    dt = _softplus(dt_raw + dtb_ref[...])
    a_neg = -jnp.exp(alog_ref[...])
    adt = a_neg * dt
    if rows < q:
        valid = lax.broadcasted_iota(jnp.int32, (q, 1), 0) < rows
        act = jnp.where(valid, act, 0.0)
        adt = jnp.where(valid, adt, 0.0)
    tril = (row_id >= col_id).astype(F32)
    acs = jnp.dot(tril, adt, precision=HIGHEST, preferred_element_type=F32)
    acs_t = acs.T
    dt_t = dt.T
    xs_t = act[:, :M_INNER].T
    upper = row_id <= col_id

    for g in range(M_GROUPS):
        b_g = act[:, M_INNER + g * M_STATE:M_INNER + (g + 1) * M_STATE]
        c_g = act[:, M_INNER + (M_GROUPS + g) * M_STATE:M_INNER + (M_GROUPS + g + 1) * M_STATE]
        b_bf = b_g.astype(BF16)
        c_t = c_g.T
        c_t_bf = c_t.astype(BF16)
        cb_t = jnp.dot(b_bf, c_t_bf, preferred_element_type=F32)
        for r in range(M_HPG):
            h = g * M_HPG + r
            a_row = acs_t[h:h + 1, :]
            a_col = acs[:, h:h + 1]
            lm_t = jnp.exp(jnp.where(upper, a_row - a_col, -jnp.inf))
            m_t = (cb_t * lm_t).astype(BF16)
            cs_t = (c_t * jnp.exp(a_row)).astype(BF16)
            x_t = xs_t[h * M_HEADDIM:(h + 1) * M_HEADDIM, :]
            xdt_t = x_t * dt_t[h:h + 1, :]
            hprev = h_scr[h]
            y_t = (jnp.dot(xdt_t.astype(BF16), m_t, preferred_element_type=F32)
                   + jnp.dot(hprev.astype(BF16), cs_t, preferred_element_type=F32))
            yT_scr[h * M_HEADDIM:(h + 1) * M_HEADDIM, :] = y_t + x_t * dskip_ref[h * M_HEADDIM:(h + 1) * M_HEADDIM, :]
            a_last = acs[q - 1:q, h:h + 1]
            wrow = jnp.exp(a_last - a_row)
            s_c = jnp.dot((xdt_t * wrow).astype(BF16), b_bf, preferred_element_type=F32)
            h_scr[h] = hprev * jnp.exp(a_last) + s_c

    y = yT_scr[...].T
    if rows < q:
        y = y[0:rows, :]
    y = y * _silu(z_ref[...])
    gw = M_INNER // M_GROUPS
    parts = []
    for g in range(M_GROUPS):
        yg = y[:, g * gw:(g + 1) * gw]
        ms = jnp.mean(yg * yg, axis=-1, keepdims=True)
        parts.append(yg * lax.rsqrt(ms + RMS_EPS))
    y_ref[...] = jnp.concatenate(parts, axis=-1) * mnw_ref[...]

    @pl.when(c == nc - 1)
    def _():
        hT_ref[0] = h_scr[...]


def _mamba(p_z, p_xbc, p_dt, conv_buf, h0, lw, bsz, seq):
    q = M_CHUNK
    if seq >= q:
        rows, nc = q, seq // q
    else:
        rows, nc = seq, 1
    cbuf = jnp.pad(conv_buf, ((0, 0), (SUBLANES - (M_CONV - 1), 0), (0, 0)))
    convw = jnp.pad(lw['conv_w'], ((0, SUBLANES - M_CONV), (0, 0)))
    kern = functools.partial(_mamba_kernel, rows=rows, nc=nc)
    row_map = lambda b, c: (b * nc + c, 0)
    const2 = lambda b, c: (0, 0)
    y, h_last = pl.pallas_call(
        kern,
        grid=(bsz, nc),
        in_specs=[pl.BlockSpec((rows, M_INNER), row_map),
                  pl.BlockSpec((rows, M_CONV_DIM), row_map),
                  pl.BlockSpec((rows, LANES), row_map),
                  pl.BlockSpec((1, SUBLANES, M_CONV_DIM), lambda b, c: (b, 0, 0)),
                  pl.BlockSpec((1, M_HEADS, M_HEADDIM, M_STATE), lambda b, c: (b, 0, 0, 0)),
                  pl.BlockSpec((SUBLANES, M_CONV_DIM), const2),
                  pl.BlockSpec((1, M_CONV_DIM), const2),
                  pl.BlockSpec((1, LANES), const2),
                  pl.BlockSpec((1, LANES), const2),
                  pl.BlockSpec((M_INNER, LANES), const2),
                  pl.BlockSpec((1, M_INNER), const2)],
        out_specs=[pl.BlockSpec((rows, M_INNER), row_map),
                   pl.BlockSpec((1, M_HEADS, M_HEADDIM, M_STATE), lambda b, c: (b, 0, 0, 0))],
        out_shape=[jax.ShapeDtypeStruct((bsz * seq, M_INNER), F32),
                   jax.ShapeDtypeStruct((bsz, M_HEADS, M_HEADDIM, M_STATE), F32)],
        scratch_shapes=[pltpu.VMEM((q + 2 * SUBLANES, M_CONV_DIM), F32),
                        pltpu.VMEM((M_HEADS, M_HEADDIM, M_STATE), F32),
                        pltpu.VMEM((M_INNER, q), F32)],
        compiler_params=_params("arbitrary", "arbitrary"),
        name="mamba_ssd",
    )(p_z, p_xbc, p_dt, cbuf, h0, convw, lw['conv_b'][None, :], lw['dt_bias_pad'], lw['a_log_pad'],
      lw['d_skip_t'], lw['m_norm_w'][None, :])
    return y, h_last


def _rwkv_prep_kernel(cols_ref, shift_ref, mu_ref, w0_ref, w2_ref, a0_ref, a2_ref, g2_ref, kkw_ref, kaw_ref,
                      r_out, w_out, k_out, v_out, a_out, kk_out, g_out, sbuf, *, tl, nt):
    t = pl.program_id(1)

    @pl.when(t == 0)
    def _():
        sbuf[0:SUBLANES, :] = shift_ref[0]

    cols = cols_ref[...]
    sbuf[SUBLANES:SUBLANES + tl, :] = cols
    prev = sbuf[SUBLANES - 1:SUBLANES - 1 + tl, :]
    if nt > 1:
        sbuf[0:SUBLANES, :] = sbuf[tl:tl + SUBLANES, :]
    mixed = cols + (prev - cols) * mu_ref[...]
    r = mixed[:, 0:D]
    k = mixed[:, D:2 * D]
    v = mixed[:, 2 * D:3 * D]
    wa = mixed[:, 3 * D:3 * D + R_LORA_WA]
    gl = mixed[:, 3 * D + R_LORA_WA:]
    lw_ = jnp.dot(jnp.tanh(wa).astype(BF16), w2_ref[...], preferred_element_type=F32)
    la_ = jnp.dot(wa.astype(BF16), a2_ref[...], preferred_element_type=F32)
    g = jnp.dot(_sigmoid(gl).astype(BF16), g2_ref[...], preferred_element_type=F32)
    w_log = -_softplus(-(w0_ref[...] + lw_)) - 0.5
    decay = jnp.exp(-jnp.exp(w_log))
    a = _sigmoid(a0_ref[...] + la_)
    r_out[...] = r
    w_out[...] = decay
    k_out[...] = k * (1.0 + (a - 1.0) * kaw_ref[...])
    v_out[...] = v
    a_out[...] = a
    kk_out[...] = k * kkw_ref[...]
    g_out[...] = g


def _rwkv_prep(p_rwkv, shift_buf, lw, bsz, seq):
    tl = min(seq, 256)
    nt = seq // tl
    sb = jnp.pad(shift_buf[:, None, :], ((0, 0), (SUBLANES - 1, 0), (0, 0)))
    row_map = lambda b, t: (b * nt + t, 0)
    const2 = lambda b, t: (0, 0)
    vec = pl.BlockSpec((1, D), const2)
    outs = pl.pallas_call(
        functools.partial(_rwkv_prep_kernel, tl=tl, nt=nt),
        grid=(bsz, nt),
        in_specs=[pl.BlockSpec((tl, R_SHIFT_W), row_map),
                  pl.BlockSpec((1, SUBLANES, R_SHIFT_W), lambda b, t: (b, 0, 0)),
                  pl.BlockSpec((1, R_SHIFT_W), const2),
                  vec, pl.BlockSpec((R_LORA_WA, D), const2),
                  vec, pl.BlockSpec((R_LORA_WA, D), const2),
                  pl.BlockSpec((R_LORA_G, D), const2),
                  vec, vec],
        out_specs=[pl.BlockSpec((tl, D), row_map)] * 7,
        out_shape=[jax.ShapeDtypeStruct((bsz * seq, D), F32)] * 7,
        scratch_shapes=[pltpu.VMEM((tl + 2 * SUBLANES, R_SHIFT_W), F32)],
        compiler_params=_params("arbitrary", "arbitrary"),
        name="rwkv_prep",
    )(p_rwkv, sb, lw['r_mu'][None, :], lw['r_w0'][None, :], lw['r_w2_pad'], lw['r_a0'][None, :], lw['r_a2_pad'],
      lw['r_g2_bf'], lw['r_kk'][None, :], lw['r_ka'][None, :])
    return outs


def _wkv_kernel(r_ref, w_ref, k_ref, v_ref, a_ref, kk_ref, s0_ref, lnw_ref, lnb_ref, rk_ref,
                y_ref, sT_ref, s_scr, *, tb, nb):
    n = R_HS
    tblk = pl.program_id(1)

    @pl.when(tblk == 0)
    def _():
        s_scr[...] = s0_ref[0]

    def step(t, carry):
        r = r_ref[0, t]
        w = w_ref[0, t]
        k = k_ref[0, t]
        a = a_ref[0, t]
        kkr = kk_ref[0, t]
        nrm = jnp.sqrt(jnp.sum(kkr * kkr, axis=0, keepdims=True))
        kk = kkr / jnp.maximum(nrm, 1e-12)
        a_vec = -kk
        b_vec = kk * a

        def igroup(ig, carry2):
            i0 = pl.multiple_of(ig * SUBLANES, SUBLANES)
            vblk = v_ref[0, t, pl.ds(i0, SUBLANES), :]
            ys = []
            for ii in range(SUBLANES):
                s_i = s_scr[i0 + ii]
                sa = jnp.sum(s_i * a_vec, axis=0, keepdims=True)
                s_new = s_i * w + sa * b_vec + vblk[ii:ii + 1, :] * k
                s_scr[i0 + ii] = s_new
                ys.append(jnp.sum(s_new * r, axis=0, keepdims=True))
            y_ref[0, t, pl.ds(i0, SUBLANES), :] = jnp.concatenate(ys, axis=0)
            return carry2

        lax.fori_loop(0, n // SUBLANES, igroup, 0)
        y = y_ref[0, t]
        mu = jnp.mean(y, axis=0, keepdims=True)
        d = y - mu
        var = jnp.mean(d * d, axis=0, keepdims=True)
        yn = d * lax.rsqrt(var + R_GN_EPS) * lnw_ref[...] + lnb_ref[...]
        bonus = jnp.sum(r * k * rk_ref[...], axis=0, keepdims=True) * v_ref[0, t]
        y_ref[0, t] = yn + bonus
        return carry

    lax.fori_loop(0, tb, step, 0)

    @pl.when(tblk == nb - 1)
    def _():
        sT_ref[0] = s_scr[...]


def _to_chain(x, bsz, seq):
    g = bsz * R_HEADS // LANES
    per = LANES // R_HEADS
    return x.reshape(g, per, seq, R_HEADS, R_HS).transpose(0, 2, 4, 1, 3).reshape(g, seq, R_HS, LANES)


def _from_chain(y, bsz, seq):
    g = bsz * R_HEADS // LANES
    per = LANES // R_HEADS
    return y.reshape(g, seq, R_HS, per, R_HEADS).transpose(0, 3, 1, 4, 2).reshape(bsz * seq, D)


def _wkv(r, w, k, v, a, kk, s0, lw, bsz, seq):
    g = bsz * R_HEADS // LANES
    per = LANES // R_HEADS
    tb = min(seq, 16)
    nb = seq // tb
    chain = [_to_chain(t, bsz, seq) for t in (r, w, k, v, a, kk)]
    s0c = s0.reshape(g, per, R_HEADS, R_HS, R_HS).transpose(0, 3, 4, 1, 2).reshape(g, R_HS, R_HS, LANES)
    blk = pl.BlockSpec((1, tb, R_HS, LANES), lambda gi, ti: (gi, ti, 0, 0))
    sblk = pl.BlockSpec((1, R_HS, R_HS, LANES), lambda gi, ti: (gi, 0, 0, 0))
    cblk = pl.BlockSpec((R_HS, LANES), lambda gi, ti: (0, 0))
    y, s_last = pl.pallas_call(
        functools.partial(_wkv_kernel, tb=tb, nb=nb),
        grid=(g, nb),
        in_specs=[blk] * 6 + [sblk, cblk, cblk, cblk],
        out_specs=[blk, sblk],
        out_shape=[jax.ShapeDtypeStruct((g, seq, R_HS, LANES), F32),
                   jax.ShapeDtypeStruct((g, R_HS, R_HS, LANES), F32)],
        scratch_shapes=[pltpu.VMEM((R_HS, R_HS, LANES), F32)],
        compiler_params=_params("arbitrary", "arbitrary"),
        name="wkv7",
    )(*chain, s0c, lw['lnx_w_c'], lw['lnx_b_c'], lw['rk_c'])
    s_out = s_last.reshape(g, R_HS, R_HS, per, R_HEADS).transpose(0, 3, 4, 1, 2).reshape(bsz, R_HEADS, R_HS, R_HS)
    return _from_chain(y, bsz, seq), s_out


def _attn_kernel(q_ref, k_ref, v_ref, o_ref):
    q = q_ref[...]
    scale = X_HEADDIM ** -0.5
    outs = []
    for h in range(X_HEADS):
        sl = slice(h * X_HEADDIM, (h + 1) * X_HEADDIM)
        qh = q[:, sl].astype(BF16)
        kh = k_ref[0, :, sl].astype(BF16)
        vh = v_ref[0, :, sl].astype(BF16)
        s = lax.dot_general(qh, kh, (((1,), (1,)), ((), ())), preferred_element_type=F32) * scale
        s = s - jnp.max(s, axis=-1, keepdims=True)
        e = jnp.exp(s)
        p = e / jnp.sum(e, axis=-1, keepdims=True)
        outs.append(jnp.dot(p.astype(BF16), vh, preferred_element_type=F32))
    o_ref[...] = jnp.concatenate(outs, axis=-1)


def _attend(p_q, mem_k, mem_v, bsz, seq):
    tl = min(seq, 512)
    nt = seq // tl
    return pl.pallas_call(
        _attn_kernel,
        grid=(bsz, nt),
        in_specs=[pl.BlockSpec((tl, D), lambda b, t: (b * nt + t, 0)),
                  pl.BlockSpec((1, MEM_LEN, D), lambda b, t: (b, 0, 0)),
                  pl.BlockSpec((1, MEM_LEN, D), lambda b, t: (b, 0, 0))],
        out_specs=pl.BlockSpec((tl, D), lambda b, t: (b * nt + t, 0)),
        out_shape=jax.ShapeDtypeStruct((bsz * seq, D), F32),
        compiler_params=_params("arbitrary", "arbitrary"),
        name="mem_attn",
    )(p_q, mem_k, mem_v)


def _merge_kernel(x_ref, ya_ref, yb_ref, g_ref, yc_ref, gate_ref, wmo_ref, wro_ref, wxo_ref, wo_ref,
                  lng_ref, lnb_ref, wr_ref, br_ref, x1_ref, lg_ref):
    gate = gate_ref[...]
    ma = jnp.dot(ya_ref[...].astype(BF16), wmo_ref[...], preferred_element_type=F32)
    mb = jnp.dot((yb_ref[...] * g_ref[...]).astype(BF16), wro_ref[...], preferred_element_type=F32)
    mc = jnp.dot(yc_ref[...].astype(BF16), wxo_ref[...], preferred_element_type=F32)
    merged = (_sigmoid(gate[:, 0:D]) * ma + _sigmoid(gate[:, D:2 * D]) * mb) + _sigmoid(gate[:, 2 * D:3 * D]) * mc
    h = ALPHA * x_ref[...] + jnp.dot(merged.astype(BF16), wo_ref[...], preferred_element_type=F32)
    x1 = _layer_norm(h, lng_ref[...], lnb_ref[...])
    x1_ref[...] = x1
    lg_ref[...] = jnp.dot(x1, wr_ref[...], precision=HIGHEST, preferred_element_type=F32) + br_ref[...]


def _merge(x, y_a, y_b, g_b, y_c, p_gate, lw):
    m = x.shape[0]
    tm = min(m, 256)
    row = lambda w: pl.BlockSpec((tm, w), lambda i: (i, 0))
    full = lambda a, b: pl.BlockSpec((a, b), lambda i: (0, 0))
    return pl.pallas_call(
        _merge_kernel,
        grid=(m // tm,),
        in_specs=[row(D), row(M_INNER), row(D), row(D), row(D), row(3 * D),
                  full(M_INNER, D), full(D, D), full(D, D), full(D, D),
                  full(1, D), full(1, D), full(D, LANES), full(1, LANES)],
        out_specs=[row(D), row(LANES)],
        out_shape=[jax.ShapeDtypeStruct((m, D), F32), jax.ShapeDtypeStruct((m, LANES), F32)],
        compiler_params=_params("arbitrary"),
        name="merge_ln_router",
    )(x, y_a, y_b, g_b, y_c, p_gate, lw['w_mo_bf'], lw['w_ro_bf'], lw['w_xo_bf'], lw['w_o_bf'],
      lw['ln1_g'][None, :], lw['ln1_b'][None, :], lw['w_router'], lw['b_router'])


def _moe_kernel(be_ref, nu_ref, x_ref, wg_ref, wu_ref, wd_ref, o_ref):
    i = pl.program_id(0)

    @pl.when(i < nu_ref[0])
    def _():
        xb = x_ref[...].astype(BF16)
        hid = _silu(jnp.dot(xb, wg_ref[0], preferred_element_type=F32)) * jnp.dot(xb, wu_ref[0],
                                                                                   preferred_element_type=F32)
        o_ref[...] = jnp.dot(hid.astype(BF16), wd_ref[0], preferred_element_type=F32)

    @pl.when(i >= nu_ref[0])
    def _():
        o_ref[...] = jnp.zeros(o_ref.shape, F32)


def _moe_experts(xb, blk_exp, n_used, lw, blk):
    n_blk = xb.shape[0] // blk
    grid_spec = pltpu.PrefetchScalarGridSpec(
        num_scalar_prefetch=2,
        grid=(n_blk,),
        in_specs=[pl.BlockSpec((blk, D), lambda i, be, nu: (i, 0)),
                  pl.BlockSpec((1, D, D_EXPERT), lambda i, be, nu: (be[i], 0, 0)),
                  pl.BlockSpec((1, D, D_EXPERT), lambda i, be, nu: (be[i], 0, 0)),
                  pl.BlockSpec((1, D_EXPERT, D), lambda i, be, nu: (be[i], 0, 0))],
        out_specs=pl.BlockSpec((blk, D), lambda i, be, nu: (i, 0)),
    )
    return pl.pallas_call(
        _moe_kernel,
        grid_spec=grid_spec,
        out_shape=jax.ShapeDtypeStruct((n_blk * blk, D), F32),
        compiler_params=_params("arbitrary"),
        name="moe_experts",
    )(blk_exp, n_used, xb, lw['w_gate_bf'], lw['w_up_bf'], lw['w_down_bf'])


def _combine_kernel(x_ref, y0_ref, y1_ref, gt_ref, lng_ref, lnb_ref, o_ref):
    gt = gt_ref[...]
    moe = gt[:, 0:1] * y0_ref[...] + gt[:, 1:2] * y1_ref[...]
    o_ref[...] = _layer_norm(ALPHA * x_ref[...] + moe, lng_ref[...], lnb_ref[...])


def _combine(x1, y0, y1, gates, lw):
    m = x1.shape[0]
    tm = min(m, 512)
    row = lambda w: pl.BlockSpec((tm, w), lambda i: (i, 0))
    full = lambda a, b: pl.BlockSpec((a, b), lambda i: (0, 0))
    return pl.pallas_call(
        _combine_kernel,
        grid=(m // tm,),
        in_specs=[row(D), row(D), row(D), row(LANES), full(1, D), full(1, D)],
        out_specs=row(D),
        out_shape=jax.ShapeDtypeStruct((m, D), F32),
        compiler_params=_params("arbitrary"),
        name="moe_combine_ln",
    )(x1, y0, y1, gates, lw['ln2_g'][None, :], lw['ln2_b'][None, :])


def _hier_moe_ln(x1, logits, lw):
    t = x1.shape[0]
    blk = 256 if t >= 8192 else 128
    lg = logits[:, :N_EGROUPS]
    le = logits[:, N_EGROUPS:N_EGROUPS + N_EXPERTS].reshape(t, N_EGROUPS, E_PER_GROUP)
    g_sel = jnp.argmax(lg, axis=-1).astype(jnp.int32)
    g_prob = jnp.take_along_axis(jax.nn.softmax(lg, axis=-1), g_sel[:, None], axis=-1)
    le = jnp.take_along_axis(le, g_sel[:, None, None], axis=1)[:, 0]
    top_v, top_i = lax.top_k(le, 2)
    gate = g_prob * jax.nn.softmax(top_v, axis=-1)
    flat_e = (g_sel[:, None] * E_PER_GROUP + top_i.astype(jnp.int32)).reshape(-1)
    n_assign = 2 * t
    order = jnp.argsort(flat_e).astype(jnp.int32)
    onehot = (flat_e[:, None] == jnp.arange(N_EXPERTS, dtype=jnp.int32)[None, :]).astype(jnp.int32)
    seen = jnp.cumsum(onehot, axis=0)
    counts = seen[-1]
    rank = jnp.sum(seen * onehot, axis=1) - 1
    padded = (counts + blk - 1) // blk * blk
    pend = jnp.cumsum(padded)
    pstart = pend - padded
    cstart = jnp.cumsum(counts) - counts
    dest = pstart[flat_e] + rank
    n_blk = n_assign // blk + N_EXPERTS
    blk_exp = jnp.minimum(jnp.searchsorted(pend, jnp.arange(n_blk, dtype=jnp.int32) * blk, side='right'),
                          N_EXPERTS - 1).astype(jnp.int32)
    slot = jnp.arange(n_blk * blk, dtype=jnp.int32)
    slot_e = jnp.repeat(blk_exp, blk)
    off = slot - pstart[slot_e]
    src = order[jnp.clip(cstart[slot_e] + off, 0, n_assign - 1)] // 2
    slot_ok = (off < counts[slot_e]) & (slot < pend[-1])
    slot_tok = jnp.where(slot_ok, src, 0)
    xb = x1[slot_tok]
    n_used = (pend[-1] // blk).astype(jnp.int32)[None]
    yb = _moe_experts(xb, blk_exp, n_used, lw, blk)
    dest2 = dest.reshape(t, 2)
    gates = jnp.pad(gate, ((0, 0), (0, LANES - 2)))
    return _combine(x1, yb[dest2[:, 0]], yb[dest2[:, 1]], gates, lw)


def _trunk_layer(x, lw, mem_k, mem_v, conv_buf, ssm_h, shift_buf, wkv_s, bsz, seq):
    p_z = _proj(x, lw['w_in_z'], 1024, 512)
    p_xbc = _proj(x, lw['w_in_xbc'], 1024, 512)
    p_dt = _proj(x, lw['w_in_dt'], 1024, LANES)
    p_rwkv = _proj(x, lw['w_in_rwkv'], 512, R_SHIFT_W // 2)
    p_q = _proj(x, lw['w_in_q'], 1024, 512)
    p_gate = _proj(x, lw['w_in_gate'], 1024, 512)

    y_a, ssm_new = _mamba(p_z, p_xbc, p_dt, conv_buf, ssm_h, lw, bsz, seq)
    conv_new = p_xbc.reshape(bsz, seq, M_CONV_DIM)[:, seq - (M_CONV - 1):]
    r, w, k, v, a, kk, g_b = _rwkv_prep(p_rwkv, shift_buf, lw, bsz, seq)
    y_b, wkv_new = _wkv(r, w, k, v, a, kk, wkv_s, lw, bsz, seq)
    shift_new = p_rwkv.reshape(bsz, seq, R_SHIFT_W)[:, -1]
    y_c = _attend(p_q, mem_k, mem_v, bsz, seq)
    x1, logits = _merge(x, y_a, y_b, g_b, y_c, p_gate, lw)
    x2 = _hier_moe_ln(x1, logits, lw)
    return x2, conv_new, ssm_new, shift_new, wkv_new


def _layer_weights(l, w):
    w_in = w['w_in'][l]
    lw = {}
    lw['w_in_z'] = w_in[:, :OFF_XBC].astype(BF16)
    lw['w_in_xbc'] = w_in[:, OFF_XBC:OFF_DT].astype(BF16)
    lw['w_in_dt'] = jnp.pad(w_in[:, OFF_DT:OFF_RWKV], ((0, 0), (0, LANES - M_HEADS))).astype(BF16)
    lw['w_in_rwkv'] = w_in[:, OFF_RWKV:OFF_Q].astype(BF16)
    lw['w_in_q'] = w_in[:, OFF_Q:OFF_GATE].astype(BF16)
    lw['w_in_gate'] = w_in[:, OFF_GATE:].astype(BF16)
    lw['conv_w'] = w['conv_w'][l]
    lw['conv_b'] = w['conv_b'][l]
    lw['dt_bias_pad'] = jnp.pad(w['dt_bias'][l], (0, LANES - M_HEADS))[None, :]
    lw['a_log_pad'] = jnp.pad(w['a_log'][l], (0, LANES - M_HEADS))[None, :]
    lw['d_skip_t'] = jnp.broadcast_to(jnp.repeat(w['d_skip'][l], M_HEADDIM)[:, None], (M_INNER, LANES))
    lw['m_norm_w'] = w['m_norm_w'][l]
    lw['r_mu'] = w['r_mu'][l]
    lw['r_w0'] = w['r_w0'][l]
    lw['r_a0'] = w['r_a0'][l]
    zeros64 = jnp.zeros((R_LORA_WA // 2, D), F32)
    lw['r_w2_pad'] = jnp.concatenate([w['r_w2'][l], zeros64], axis=0).astype(BF16)
    lw['r_a2_pad'] = jnp.concatenate([zeros64, w['r_a2'][l]], axis=0).astype(BF16)
    lw['r_g2_bf'] = w['r_g2'][l].astype(BF16)
    lw['r_kk'] = w['r_kk'][l]
    lw['r_ka'] = w['r_ka'][l]
    per = LANES // R_HEADS
    chainify = lambda t: jnp.tile(t.reshape(R_HEADS, R_HS).T, (1, per))
    lw['lnx_w_c'] = chainify(w['r_lnx_w'][l])
    lw['lnx_b_c'] = chainify(w['r_lnx_b'][l])
    lw['rk_c'] = chainify(w['r_rk'][l].reshape(-1))
    lw['w_mem_kv_bf'] = w['w_mem_kv'][l].astype(BF16)
    lw['w_mo_bf'] = w['w_mo'][l].astype(BF16)
    lw['w_ro_bf'] = w['w_ro'][l].astype(BF16)
    lw['w_xo_bf'] = w['w_xo'][l].astype(BF16)
    lw['w_o_bf'] = w['w_o'][l].astype(BF16)
    lw['ln1_g'] = w['ln1_g'][l]
    lw['ln1_b'] = w['ln1_b'][l]
    lw['ln2_g'] = w['ln2_g'][l]
    lw['ln2_b'] = w['ln2_b'][l]
    nr = N_EGROUPS + N_EXPERTS
    lw['w_router'] = jnp.pad(jnp.concatenate([w['w_rg'][l], w['w_re'][l]], axis=1), ((0, 0), (0, LANES - nr)))
    lw['b_router'] = jnp.pad(jnp.concatenate([w['b_rg'][l], w['b_re'][l]]), (0, LANES - nr))[None, :]
    lw['w_gate_bf'] = w['w_gate'][l].astype(BF16)
    lw['w_up_bf'] = w['w_up'][l].astype(BF16)
    lw['w_down_bf'] = w['w_down'][l].astype(BF16)
    return lw


def kernel(x_prompt, x_sample, mem_prompt, state_ssm, state_conv, state_wkv, state_shift, cache_mem_k, cache_mem_v, w_in, conv_w, conv_b, dt_bias, a_log, d_skip, m_norm_w, r_mu, r_w0, r_w2, r_a0, r_a2, r_g2, r_kk, r_ka, r_rk, r_lnx_w, r_lnx_b, w_mem_kv, w_mo, w_ro, w_xo, w_o, ln1_g, ln1_b, w_rg, b_rg, w_re, b_re, w_gate, w_up, w_down, ln2_g, ln2_b):
    w = dict(w_in=w_in, conv_w=conv_w, conv_b=conv_b, dt_bias=dt_bias, a_log=a_log, d_skip=d_skip,
             m_norm_w=m_norm_w, r_mu=r_mu, r_w0=r_w0, r_w2=r_w2, r_a0=r_a0, r_a2=r_a2, r_g2=r_g2, r_kk=r_kk,
             r_ka=r_ka, r_rk=r_rk, r_lnx_w=r_lnx_w, r_lnx_b=r_lnx_b, w_mem_kv=w_mem_kv, w_mo=w_mo, w_ro=w_ro,
             w_xo=w_xo, w_o=w_o, ln1_g=ln1_g, ln1_b=ln1_b, w_rg=w_rg, b_rg=b_rg, w_re=w_re, b_re=b_re,
             w_gate=w_gate, w_up=w_up, w_down=w_down, ln2_g=ln2_g, ln2_b=ln2_b)
    bp, sp, _ = x_prompt.shape
    bs, ss, _ = x_sample.shape
    n_mem = mem_prompt.shape[1]
    xp = x_prompt.reshape(bp * sp, D)
    xs = x_sample.reshape(bs * ss, D)
    mem_flat = mem_prompt.reshape(bp * n_mem, D)
    zero_conv = jnp.zeros((bp, M_CONV - 1, M_CONV_DIM), F32)
    zero_ssm = jnp.zeros((bp, M_HEADS, M_HEADDIM, M_STATE), F32)
    zero_shift = jnp.zeros((bp, R_SHIFT_W), F32)
    zero_wkv = jnp.zeros((bp, R_HEADS, R_HS, R_HS), F32)
    outs = {n: [] for n in ('p_ssm', 'p_conv', 'p_wkv', 'p_shift', 'p_mk', 'p_mv', 's_ssm', 's_conv', 's_wkv',
                            's_shift')}
    for l in range(DEPTH):
        lw = _layer_weights(l, w)
        mkv = _proj(mem_flat, lw['w_mem_kv_bf'], 1024, 512)
        mk = mkv[:, :D].reshape(bp, n_mem, D)
        mv = mkv[:, D:].reshape(bp, n_mem, D)
        xp, cp, hp, shp, wp = _trunk_layer(xp, lw, mk, mv, zero_conv, zero_ssm, zero_shift, zero_wkv, bp, sp)
        xs, cs, hs, shs, ws = _trunk_layer(xs, lw, cache_mem_k[l].reshape(bs, n_mem, D),
                                           cache_mem_v[l].reshape(bs, n_mem, D), state_conv[l], state_ssm[l],
                                           state_shift[l], state_wkv[l], bs, ss)
        outs['p_ssm'].append(hp)
        outs['p_conv'].append(cp)
        outs['p_wkv'].append(wp)
        outs['p_shift'].append(shp)
        outs['p_mk'].append(mk.reshape(bp, n_mem, X_HEADS, X_HEADDIM))
        outs['p_mv'].append(mv.reshape(bp, n_mem, X_HEADS, X_HEADDIM))
        outs['s_ssm'].append(hs)
        outs['s_conv'].append(cs)
        outs['s_wkv'].append(ws)
        outs['s_shift'].append(shs)
    st = lambda n: jnp.stack(outs[n])
    return (xp.reshape(bp, sp, D), xs.reshape(bs, ss, D), st('p_ssm'), st('p_conv'), st('p_wkv'), st('p_shift'),
            st('p_mk'), st('p_mv'), st('s_ssm'), st('s_conv'), st('s_wkv'), st('s_shift'))
```

```python
import functools

import jax
import jax.numpy as jnp
from jax import lax
from jax.experimental import pallas as pl
from jax.experimental.pallas import tpu as pltpu

F32 = jnp.float32
BF16 = jnp.bfloat16
HIGHEST = lax.Precision.HIGHEST

LANES = 128
SUBLANES = 8
VMEM_LIMIT = 56 * 1024 * 1024

D = 1024
DEPTH = 4
M_INNER = 2048
M_HEADDIM = 64
M_HEADS = 32
M_GROUPS = 8
M_HPG = M_HEADS // M_GROUPS
M_STATE = 128
M_CONV = 4
M_CONV_DIM = M_INNER + 2 * M_GROUPS * M_STATE
M_CHUNK = 128
RMS_EPS = 1e-5
R_HEADS = 16
R_HS = 64
R_LORA_WA = 128
R_LORA_G = 128
R_SHIFT_W = 3 * D + R_LORA_WA + R_LORA_G
R_GN_EPS = 64e-5
MEM_LEN = 256
X_HEADS = 4
X_HEADDIM = 256
OFF_XBC = M_INNER
OFF_DT = OFF_XBC + M_CONV_DIM
OFF_RWKV = OFF_DT + M_HEADS
OFF_Q = OFF_RWKV + R_SHIFT_W
OFF_GATE = OFF_Q + D
N_EGROUPS = 4
E_PER_GROUP = 8
N_EXPERTS = 32
D_EXPERT = 512
LN_EPS = 1e-5
ALPHA = (2 * DEPTH) ** 0.25


def _params(*sem):
    return pltpu.CompilerParams(dimension_semantics=sem, vmem_limit_bytes=VMEM_LIMIT)


def _sigmoid(x):
    return 1.0 / (1.0 + jnp.exp(-x))


def _silu(x):
    return x * _sigmoid(x)


def _softplus(x):
    return jnp.maximum(x, 0.0) + jnp.log1p(jnp.exp(-jnp.abs(x)))


def _layer_norm(h, g, b):
    mu = jnp.mean(h, axis=-1, keepdims=True)
    d = h - mu
    var = jnp.mean(d * d, axis=-1, keepdims=True)
    return d * lax.rsqrt(var + LN_EPS) * g + b


def _proj_kernel(x_ref, w_ref, o_ref):
    o_ref[...] = jnp.dot(x_ref[...], w_ref[...], preferred_element_type=F32)


def _proj(x, w, tm, tn):
    m, k = x.shape
    n = w.shape[1]
    tm = min(tm, m)
    tn = min(tn, n)
    return pl.pallas_call(
        _proj_kernel,
        grid=(n // tn, m // tm),
        in_specs=[pl.BlockSpec((tm, k), lambda j, i: (i, 0)),
                  pl.BlockSpec((k, tn), lambda j, i: (0, j))],
        out_specs=pl.BlockSpec((tm, tn), lambda j, i: (i, j)),
        out_shape=jax.ShapeDtypeStruct((m, n), F32),
        compiler_params=_params("arbitrary", "arbitrary"),
        name="proj",
    )(x, w)


def _mamba_kernel(z_ref, xbc_ref, dt_ref, cbuf_ref, h0_ref, convw_ref, convb_ref, dtb_ref, alog_ref,
                  dskip_ref, mnw_ref, y_ref, hT_ref, ubuf, h_scr, yT_scr, *, rows, nc):
    q = M_CHUNK
    c = pl.program_id(1)

    @pl.when(c == 0)
    def _():
        if rows < q:
            ubuf[...] = jnp.zeros(ubuf.shape, F32)
        ubuf[0:SUBLANES, :] = cbuf_ref[0]
        h_scr[...] = h0_ref[0, 0]

    ubuf[SUBLANES:SUBLANES + rows, :] = xbc_ref[...]
    cw = convw_ref[...]
    acc = ubuf[5:5 + q, :] * cw[0:1]
    acc = acc + ubuf[6:6 + q, :] * cw[1:2]
    acc = acc + ubuf[7:7 + q, :] * cw[2:3]
    acc = acc + ubuf[8:8 + q, :] * cw[3:4]
    acc = acc + convb_ref[...]
    if nc > 1:
        ubuf[0:SUBLANES, :] = ubuf[q:q + SUBLANES, :]
    act = _silu(acc)

    row_id = lax.broadcasted_iota(jnp.int32, (q, q), 0)
    col_id = lax.broadcasted_iota(jnp.int32, (q, q), 1)
    dt_raw = dt_ref[...]
    if rows < q:
        dt_raw = jnp.concatenate([dt_raw, jnp.zeros((q - rows, LANES), F32)], axis=0)
    dt = _softplus(dt_raw + dtb_ref[...])
    a_neg = -jnp.exp(alog_ref[...])
    adt = a_neg * dt
    if rows < q:
        valid = lax.broadcasted_iota(jnp.int32, (q, 1), 0) < rows
        act = jnp.where(valid, act, 0.0)
        adt = jnp.where(valid, adt, 0.0)
    tril = (row_id >= col_id).astype(F32)
    acs = jnp.dot(tril, adt, precision=HIGHEST, preferred_element_type=F32)
    acs_t = acs.T
    dt_t = dt.T
    xs_t = act[:, :M_INNER].T
    upper = row_id <= col_id

    for g in range(M_GROUPS):
        b_g = act[:, M_INNER + g * M_STATE:M_INNER + (g + 1) * M_STATE]
        c_g = act[:, M_INNER + (M_GROUPS + g) * M_STATE:M_INNER + (M_GROUPS + g + 1) * M_STATE]
        b_bf = b_g.astype(BF16)
        c_t = c_g.T
        c_t_bf = c_t.astype(BF16)
        cb_t = jnp.dot(b_bf, c_t_bf, preferred_element_type=F32)
        for r in range(M_HPG):
            h = g * M_HPG + r
            a_row = acs_t[h:h + 1, :]
            a_col = acs[:, h:h + 1]
            lm_t = jnp.exp(jnp.where(upper, a_row - a_col, -jnp.inf))
            m_t = (cb_t * lm_t).astype(BF16)
            cs_t = (c_t * jnp.exp(a_row)).astype(BF16)
            x_t = xs_t[h * M_HEADDIM:(h + 1) * M_HEADDIM, :]
            xdt_t = x_t * dt_t[h:h + 1, :]
            hprev = h_scr[h]
            y_t = (jnp.dot(xdt_t.astype(BF16), m_t, preferred_element_type=F32)
                   + jnp.dot(hprev.astype(BF16), cs_t, preferred_element_type=F32))
            yT_scr[h * M_HEADDIM:(h + 1) * M_HEADDIM, :] = y_t + x_t * dskip_ref[h * M_HEADDIM:(h + 1) * M_HEADDIM, :]
            a_last = acs[q - 1:q, h:h + 1]
            wrow = jnp.exp(a_last - a_row)
            s_c = jnp.dot((xdt_t * wrow).astype(BF16), b_bf, preferred_element_type=F32)
            h_scr[h] = hprev * jnp.exp(a_last) + s_c

    y = yT_scr[...].T
    if rows < q:
        y = y[0:rows, :]
    y = y * _silu(z_ref[...])
    gw = M_INNER // M_GROUPS
    parts = []
    for g in range(M_GROUPS):
        yg = y[:, g * gw:(g + 1) * gw]
        ms = jnp.mean(yg * yg, axis=-1, keepdims=True)
        parts.append(yg * lax.rsqrt(ms + RMS_EPS))
    y_ref[...] = jnp.concatenate(parts, axis=-1) * mnw_ref[...]

    @pl.when(c == nc - 1)
    def _():
        hT_ref[0] = h_scr[...]


def _mamba(p_z, p_xbc, p_dt, conv_buf, h0_all, layer, lw, bsz, seq):
    q = M_CHUNK
    if seq >= q:
        rows, nc = q, seq // q
    else:
        rows, nc = seq, 1
    if conv_buf is None:
        cbuf = jnp.zeros((1, SUBLANES, M_CONV_DIM), F32)
        cbuf_map = lambda b, c: (0, 0, 0)
    else:
        cbuf = jnp.pad(conv_buf, ((0, 0), (SUBLANES - (M_CONV - 1), 0), (0, 0)))
        cbuf_map = lambda b, c: (b, 0, 0)
    if h0_all is None:
        h0_all = jnp.zeros((1, 1, M_HEADS, M_HEADDIM, M_STATE), F32)
        h0_map = lambda b, c: (0, 0, 0, 0, 0)
    else:
        h0_map = lambda b, c: (layer, b, 0, 0, 0)
    convw = jnp.pad(lw['conv_w'], ((0, SUBLANES - M_CONV), (0, 0)))
    kern = functools.partial(_mamba_kernel, rows=rows, nc=nc)
    row_map = lambda b, c: (b * nc + c, 0)
    const2 = lambda b, c: (0, 0)
    y, h_last = pl.pallas_call(
        kern,
        grid=(bsz, nc),
        in_specs=[pl.BlockSpec((rows, M_INNER), row_map),
                  pl.BlockSpec((rows, M_CONV_DIM), row_map),
                  pl.BlockSpec((rows, LANES), row_map),
                  pl.BlockSpec((1, SUBLANES, M_CONV_DIM), cbuf_map),
                  pl.BlockSpec((1, 1, M_HEADS, M_HEADDIM, M_STATE), h0_map),
                  pl.BlockSpec((SUBLANES, M_CONV_DIM), const2),
                  pl.BlockSpec((1, M_CONV_DIM), const2),
                  pl.BlockSpec((1, LANES), const2),
                  pl.BlockSpec((1, LANES), const2),
                  pl.BlockSpec((M_INNER, LANES), const2),
                  pl.BlockSpec((1, M_INNER), const2)],
        out_specs=[pl.BlockSpec((rows, M_INNER), row_map),
                   pl.BlockSpec((1, M_HEADS, M_HEADDIM, M_STATE), lambda b, c: (b, 0, 0, 0))],
        out_shape=[jax.ShapeDtypeStruct((bsz * seq, M_INNER), F32),
                   jax.ShapeDtypeStruct((bsz, M_HEADS, M_HEADDIM, M_STATE), F32)],
        scratch_shapes=[pltpu.VMEM((q + 2 * SUBLANES, M_CONV_DIM), F32),
                        pltpu.VMEM((M_HEADS, M_HEADDIM, M_STATE), F32),
                        pltpu.VMEM((M_INNER, q), F32)],
        compiler_params=_params("arbitrary", "arbitrary"),
        name="mamba_ssd",
    )(p_z, p_xbc, p_dt, cbuf, h0_all, convw, lw['conv_b'][None, :], lw['dt_bias_pad'], lw['a_log_pad'],
      lw['d_skip_t'], lw['m_norm_w'][None, :])
    return y, h_last


def _rwkv_prep_kernel(cols_ref, shift_ref, mu_ref, w0_ref, w2_ref, a0_ref, a2_ref, g2_ref, kkw_ref, kaw_ref,
                      r_out, w_out, k_out, v_out, a_out, kk_out, g_out, sbuf, *, tl, nt):
    t = pl.program_id(1)

    @pl.when(t == 0)
    def _():
        sbuf[0:SUBLANES, :] = shift_ref[0]

    cols = cols_ref[...]
    sbuf[SUBLANES:SUBLANES + tl, :] = cols
    prev = sbuf[SUBLANES - 1:SUBLANES - 1 + tl, :]
    if nt > 1:
        sbuf[0:SUBLANES, :] = sbuf[tl:tl + SUBLANES, :]
    mixed = cols + (prev - cols) * mu_ref[...]
    r = mixed[:, 0:D]
    k = mixed[:, D:2 * D]
    v = mixed[:, 2 * D:3 * D]
    wa = mixed[:, 3 * D:3 * D + R_LORA_WA]
    gl = mixed[:, 3 * D + R_LORA_WA:]
    lw_ = jnp.dot(jnp.tanh(wa).astype(BF16), w2_ref[...], preferred_element_type=F32)
    la_ = jnp.dot(wa.astype(BF16), a2_ref[...], preferred_element_type=F32)
    g = jnp.dot(_sigmoid(gl).astype(BF16), g2_ref[...], preferred_element_type=F32)
    w_log = -_softplus(-(w0_ref[...] + lw_)) - 0.5
    decay = jnp.exp(-jnp.exp(w_log))
    a = _sigmoid(a0_ref[...] + la_)
    r_out[...] = r
    w_out[...] = decay
    k_out[...] = k * (1.0 + (a - 1.0) * kaw_ref[...])
    v_out[...] = v
    a_out[...] = a
    kk_out[...] = k * kkw_ref[...]
    g_out[...] = g


def _rwkv_prep(p_rwkv, shift_buf, lw, bsz, seq):
    tl = min(seq, 256)
    nt = seq // tl
    if shift_buf is None:
        sb = jnp.zeros((1, SUBLANES, R_SHIFT_W), F32)
        sb_map = lambda b, t: (0, 0, 0)
    else:
        sb = jnp.pad(shift_buf[:, None, :], ((0, 0), (SUBLANES - 1, 0), (0, 0)))
        sb_map = lambda b, t: (b, 0, 0)
    row_map = lambda b, t: (b * nt + t, 0)
    const2 = lambda b, t: (0, 0)
    vec = pl.BlockSpec((1, D), const2)
    outs = pl.pallas_call(
        functools.partial(_rwkv_prep_kernel, tl=tl, nt=nt),
        grid=(bsz, nt),
        in_specs=[pl.BlockSpec((tl, R_SHIFT_W), row_map),
                  pl.BlockSpec((1, SUBLANES, R_SHIFT_W), sb_map),
                  pl.BlockSpec((1, R_SHIFT_W), const2),
                  vec, pl.BlockSpec((R_LORA_WA, D), const2),
                  vec, pl.BlockSpec((R_LORA_WA, D), const2),
                  pl.BlockSpec((R_LORA_G, D), const2),
                  vec, vec],
        out_specs=[pl.BlockSpec((tl, D), row_map)] * 7,
        out_shape=[jax.ShapeDtypeStruct((bsz * seq, D), F32)] * 7,
        scratch_shapes=[pltpu.VMEM((tl + 2 * SUBLANES, R_SHIFT_W), F32)],
        compiler_params=_params("arbitrary", "arbitrary"),
        name="rwkv_prep",
    )(p_rwkv, sb, lw['r_mu'][None, :], lw['r_w0'][None, :], lw['r_w2_pad'], lw['r_a0'][None, :], lw['r_a2_pad'],
      lw['r_g2_bf'], lw['r_kk'][None, :], lw['r_ka'][None, :])
    return outs


def _wkv_kernel(r_ref, w_ref, k_ref, v_ref, a_ref, kk_ref, s0_ref, lnw_ref, lnb_ref, rk_ref,
                y_ref, sT_ref, s_scr, an_scr, b_scr, *, tb, nb):
    ng = R_HS // SUBLANES
    tblk = pl.program_id(1)

    @pl.when(tblk == 0)
    def _():
        s_scr[...] = s0_ref[0]

    def kk_unit(t):
        kkr = kk_ref[0, t]
        nrm = jnp.sqrt(jnp.sum(kkr * kkr, axis=0, keepdims=True))
        return kkr / jnp.maximum(nrm, 1e-12)

    def row(ref_view, j):
        return jnp.broadcast_to(ref_view[pl.ds(j, 1), :], (SUBLANES, LANES))

    zeros = tuple(jnp.zeros((SUBLANES, LANES), F32) for _ in range(ng))
    an_scr[0] = -kk_unit(0)

    def first_sa(j, acc):
        aj = row(an_scr.at[0], j)
        return tuple(acc[ig] + s_scr[ig, j] * aj for ig in range(ng))

    sa0 = lax.fori_loop(0, R_HS, first_sa, zeros)

    def step(t, sa):
        cur = t % 2
        kk = -an_scr[cur]
        b_scr[...] = kk * a_ref[0, t]
        t_next = jnp.minimum(t + 1, tb - 1)
        an_scr[1 - cur] = -kk_unit(t_next)
        v = v_ref[0, t]
        vs = tuple(v[ig * SUBLANES:(ig + 1) * SUBLANES, :] for ig in range(ng))
        w_t, k_t, r_t, an_next = w_ref.at[0, t], k_ref.at[0, t], r_ref.at[0, t], an_scr.at[1 - cur]

        def col(j, carry):
            yacc, san = carry
            wj, bj, kj, rj, aj = row(w_t, j), row(b_scr, j), row(k_t, j), row(r_t, j), row(an_next, j)
            ynew, snew = [], []
            for ig in range(ng):
                s = s_scr[ig, j] * wj + sa[ig] * bj + vs[ig] * kj
                s_scr[ig, j] = s
                ynew.append(yacc[ig] + s * rj)
                snew.append(san[ig] + s * aj)
            return tuple(ynew), tuple(snew)

        yacc, san = lax.fori_loop(0, R_HS, col, (zeros, zeros), unroll=8)
        y = jnp.concatenate(yacc, axis=0)
        mu = jnp.mean(y, axis=0, keepdims=True)
        d = y - mu
        var = jnp.mean(d * d, axis=0, keepdims=True)
        yn = d * lax.rsqrt(var + R_GN_EPS) * lnw_ref[...] + lnb_ref[...]
        bonus = jnp.sum(r_ref[0, t] * k_ref[0, t] * rk_ref[...], axis=0, keepdims=True) * v
        y_ref[0, t] = yn + bonus
        return san

    lax.fori_loop(0, tb, step, sa0)

    @pl.when(tblk == nb - 1)
    def _():
        sT_ref[0] = s_scr[...]


def _to_chain(x, bsz, seq):
    g = bsz * R_HEADS // LANES
    per = LANES // R_HEADS
    return x.reshape(g, per, seq, R_HEADS, R_HS).transpose(0, 2, 4, 1, 3).reshape(g, seq, R_HS, LANES)


def _from_chain(y, bsz, seq):
    g = bsz * R_HEADS // LANES
    per = LANES // R_HEADS
    return y.reshape(g, seq, R_HS, per, R_HEADS).transpose(0, 3, 1, 4, 2).reshape(bsz * seq, D)


def _wkv(r, w, k, v, a, kk, s0, lw, bsz, seq):
    g = bsz * R_HEADS // LANES
    per = LANES // R_HEADS
    ng = R_HS // SUBLANES
    tb = min(seq, 16)
    nb = seq // tb
    chain = [_to_chain(t, bsz, seq) for t in (r, w, k, v, a, kk)]
    st_shape = (ng, R_HS, SUBLANES, LANES)
    if s0 is None:
        s0c = jnp.zeros((1,) + st_shape, F32)
        s_in = pl.BlockSpec((1,) + st_shape, lambda gi, ti: (0, 0, 0, 0, 0))
    else:
        s0c = s0.reshape(g, per, R_HEADS, ng, SUBLANES, R_HS).transpose(0, 3, 5, 4, 1, 2).reshape((g,) + st_shape)
        s_in = pl.BlockSpec((1,) + st_shape, lambda gi, ti: (gi, 0, 0, 0, 0))
    blk = pl.BlockSpec((1, tb, R_HS, LANES), lambda gi, ti: (gi, ti, 0, 0))
    s_out_spec = pl.BlockSpec((1,) + st_shape, lambda gi, ti: (gi, 0, 0, 0, 0))
    cblk = pl.BlockSpec((R_HS, LANES), lambda gi, ti: (0, 0))
    y, s_last = pl.pallas_call(
        functools.partial(_wkv_kernel, tb=tb, nb=nb),
        grid=(g, nb),
        in_specs=[blk] * 6 + [s_in, cblk, cblk, cblk],
        out_specs=[blk, s_out_spec],
        out_shape=[jax.ShapeDtypeStruct((g, seq, R_HS, LANES), F32),
                   jax.ShapeDtypeStruct((g,) + st_shape, F32)],
        scratch_shapes=[pltpu.VMEM(st_shape, F32),
                        pltpu.VMEM((2, R_HS, LANES), F32),
                        pltpu.VMEM((R_HS, LANES), F32)],
        compiler_params=_params("arbitrary", "arbitrary"),
        name="wkv7",
    )(*chain, s0c, lw['lnx_w_c'], lw['lnx_b_c'], lw['rk_c'])
    s_out = s_last.reshape(g, ng, R_HS, SUBLANES, per, R_HEADS).transpose(0, 4, 5, 1, 3, 2)
    return _from_chain(y, bsz, seq), s_out.reshape(bsz, R_HEADS, R_HS, R_HS)


def _attn_kernel(q_ref, k_ref, v_ref, o_ref, *, head_major_cols):
    q = q_ref[...]
    scale = X_HEADDIM ** -0.5
    outs = []
    for h in range(X_HEADS):
        sl = slice(h * X_HEADDIM, (h + 1) * X_HEADDIM)
        qh = q[:, sl].astype(BF16)
        if head_major_cols:
            kh = k_ref[:, sl].astype(BF16)
            vh = v_ref[:, sl].astype(BF16)
        else:
            kh = k_ref[0, 0, :, h, :].astype(BF16)
            vh = v_ref[0, 0, :, h, :].astype(BF16)
        s = lax.dot_general(qh, kh, (((1,), (1,)), ((), ())), preferred_element_type=F32) * scale
        s = s - jnp.max(s, axis=-1, keepdims=True)
        e = jnp.exp(s)
        p = e / jnp.sum(e, axis=-1, keepdims=True)
        outs.append(jnp.dot(p.astype(BF16), vh, preferred_element_type=F32))
    o_ref[...] = jnp.concatenate(outs, axis=-1)


def _attend(p_q, mem, layer, bsz, seq):
    tl = min(seq, 512)
    nt = seq // tl
    if isinstance(mem, tuple):
        cache_k, cache_v = mem
        cblk = pl.BlockSpec((1, 1, MEM_LEN, X_HEADS, X_HEADDIM), lambda b, t: (layer, b, 0, 0, 0))
        kv_specs, kv_args, head_major_cols = [cblk, cblk], (cache_k, cache_v), False
    else:
        kv_specs = [pl.BlockSpec((MEM_LEN, D), lambda b, t: (b, 0)), pl.BlockSpec((MEM_LEN, D), lambda b, t: (b, 1))]
        kv_args, head_major_cols = (mem, mem), True
    return pl.pallas_call(
        functools.partial(_attn_kernel, head_major_cols=head_major_cols),
        grid=(bsz, nt),
        in_specs=[pl.BlockSpec((tl, D), lambda b, t: (b * nt + t, 0))] + kv_specs,
        out_specs=pl.BlockSpec((tl, D), lambda b, t: (b * nt + t, 0)),
        out_shape=jax.ShapeDtypeStruct((bsz * seq, D), F32),
        compiler_params=_params("arbitrary", "arbitrary"),
        name="mem_attn",
    )(p_q, *kv_args)


def _merge_kernel(x_ref, ya_ref, yb_ref, g_ref, yc_ref, gate_ref, wmo_ref, wro_ref, wxo_ref, wo_ref,
                  lng_ref, lnb_ref, wr_ref, br_ref, x1_ref, lg_ref):
    gate = gate_ref[...]
    ma = jnp.dot(ya_ref[...].astype(BF16), wmo_ref[...], preferred_element_type=F32)
    mb = jnp.dot((yb_ref[...] * g_ref[...]).astype(BF16), wro_ref[...], preferred_element_type=F32)
    mc = jnp.dot(yc_ref[...].astype(BF16), wxo_ref[...], preferred_element_type=F32)
    merged = (_sigmoid(gate[:, 0:D]) * ma + _sigmoid(gate[:, D:2 * D]) * mb) + _sigmoid(gate[:, 2 * D:3 * D]) * mc
    h = ALPHA * x_ref[...] + jnp.dot(merged.astype(BF16), wo_ref[...], preferred_element_type=F32)
    x1 = _layer_norm(h, lng_ref[...], lnb_ref[...])
    x1_ref[...] = x1
    lg_ref[...] = jnp.dot(x1, wr_ref[...], precision=HIGHEST, preferred_element_type=F32) + br_ref[...]


def _merge(x, y_a, y_b, g_b, y_c, p_gate, lw):
    m = x.shape[0]
    tm = min(m, 256)
    row = lambda w: pl.BlockSpec((tm, w), lambda i: (i, 0))
    full = lambda a, b: pl.BlockSpec((a, b), lambda i: (0, 0))
    return pl.pallas_call(
        _merge_kernel,
        grid=(m // tm,),
        in_specs=[row(D), row(M_INNER), row(D), row(D), row(D), row(3 * D),
                  full(M_INNER, D), full(D, D), full(D, D), full(D, D),
                  full(1, D), full(1, D), full(D, LANES), full(1, LANES)],
        out_specs=[row(D), row(LANES)],
        out_shape=[jax.ShapeDtypeStruct((m, D), F32), jax.ShapeDtypeStruct((m, LANES), F32)],
        compiler_params=_params("arbitrary"),
        name="merge_ln_router",
    )(x, y_a, y_b, g_b, y_c, p_gate, lw['w_mo_bf'], lw['w_ro_bf'], lw['w_xo_bf'], lw['w_o_bf'],
      lw['ln1_g'][None, :], lw['ln1_b'][None, :], lw['w_router'], lw['b_router'])


def _moe_kernel(be_ref, nu_ref, x_ref, wg_ref, wu_ref, wd_ref, o_ref, wg_bf, wu_bf, wd_bf):
    i = pl.program_id(0)
    active = i < nu_ref[0]
    new_expert = jnp.logical_or(i == 0, be_ref[i] != be_ref[jnp.maximum(i - 1, 0)])

    @pl.when(jnp.logical_and(active, new_expert))
    def _():
        wg_bf[...] = wg_ref[0, 0].astype(BF16)
        wu_bf[...] = wu_ref[0, 0].astype(BF16)
        wd_bf[...] = wd_ref[0, 0].astype(BF16)

    @pl.when(active)
    def _():
        xb = x_ref[...].astype(BF16)
        hid = _silu(jnp.dot(xb, wg_bf[...], preferred_element_type=F32)) * jnp.dot(xb, wu_bf[...],
                                                                                    preferred_element_type=F32)
        o_ref[...] = jnp.dot(hid.astype(BF16), wd_bf[...], preferred_element_type=F32)

    @pl.when(jnp.logical_not(active))
    def _():
        o_ref[...] = jnp.zeros(o_ref.shape, F32)


def _moe_experts(xb, blk_exp, n_used, w_gate, w_up, w_down, layer, blk):
    n_blk = xb.shape[0] // blk
    grid_spec = pltpu.PrefetchScalarGridSpec(
        num_scalar_prefetch=2,
        grid=(n_blk,),
        in_specs=[pl.BlockSpec((blk, D), lambda i, be, nu: (i, 0)),
                  pl.BlockSpec((1, 1, D, D_EXPERT), lambda i, be, nu: (layer, be[i], 0, 0)),
                  pl.BlockSpec((1, 1, D, D_EXPERT), lambda i, be, nu: (layer, be[i], 0, 0)),
                  pl.BlockSpec((1, 1, D_EXPERT, D), lambda i, be, nu: (layer, be[i], 0, 0))],
        out_specs=pl.BlockSpec((blk, D), lambda i, be, nu: (i, 0)),
        scratch_shapes=[pltpu.VMEM((D, D_EXPERT), BF16), pltpu.VMEM((D, D_EXPERT), BF16),
                        pltpu.VMEM((D_EXPERT, D), BF16)],
    )
    return pl.pallas_call(
        _moe_kernel,
        grid_spec=grid_spec,
        out_shape=jax.ShapeDtypeStruct((n_blk * blk, D), F32),
        compiler_params=_params("arbitrary"),
        name="moe_experts",
    )(blk_exp, n_used, xb, w_gate, w_up, w_down)


def _combine_kernel(x_ref, y0_ref, y1_ref, gt_ref, lng_ref, lnb_ref, o_ref, ob_ref):
    gt = gt_ref[...]
    moe = gt[:, 0:1] * y0_ref[...] + gt[:, 1:2] * y1_ref[...]
    x2 = _layer_norm(ALPHA * x_ref[...] + moe, lng_ref[...], lnb_ref[...])
    o_ref[...] = x2
    ob_ref[...] = x2.astype(BF16)


def _combine(x1, y0, y1, gates, lw):
    m = x1.shape[0]
    tm = min(m, 512)
    row = lambda w: pl.BlockSpec((tm, w), lambda i: (i, 0))
    full = lambda a, b: pl.BlockSpec((a, b), lambda i: (0, 0))
    return pl.pallas_call(
        _combine_kernel,
        grid=(m // tm,),
        in_specs=[row(D), row(D), row(D), row(LANES), full(1, D), full(1, D)],
        out_specs=[row(D), row(D)],
        out_shape=[jax.ShapeDtypeStruct((m, D), F32), jax.ShapeDtypeStruct((m, D), BF16)],
        compiler_params=_params("arbitrary"),
        name="moe_combine_ln",
    )(x1, y0, y1, gates, lw['ln2_g'][None, :], lw['ln2_b'][None, :])


def _hier_moe_ln(x1, logits, lw, w_experts, layer):
    t = x1.shape[0]
    blk = 256 if t >= 8192 else 128
    lg = logits[:, :N_EGROUPS]
    le = logits[:, N_EGROUPS:N_EGROUPS + N_EXPERTS].reshape(t, N_EGROUPS, E_PER_GROUP)
    g_sel = jnp.argmax(lg, axis=-1).astype(jnp.int32)
    g_prob = jnp.take_along_axis(jax.nn.softmax(lg, axis=-1), g_sel[:, None], axis=-1)
    le = jnp.take_along_axis(le, g_sel[:, None, None], axis=1)[:, 0]
    top_v, top_i = lax.top_k(le, 2)
    gate = g_prob * jax.nn.softmax(top_v, axis=-1)
    flat_e = (g_sel[:, None] * E_PER_GROUP + top_i.astype(jnp.int32)).reshape(-1)
    n_assign = 2 * t
    order = jnp.argsort(flat_e).astype(jnp.int32)
    onehot = (flat_e[:, None] == jnp.arange(N_EXPERTS, dtype=jnp.int32)[None, :]).astype(jnp.int32)
    seen = jnp.cumsum(onehot, axis=0)
    counts = seen[-1]
    rank = jnp.sum(seen * onehot, axis=1) - 1
    padded = (counts + blk - 1) // blk * blk
    pend = jnp.cumsum(padded)
    pstart = pend - padded
    cstart = jnp.cumsum(counts) - counts
    dest = pstart[flat_e] + rank
    n_blk = n_assign // blk + N_EXPERTS
    blk_start = jnp.arange(n_blk, dtype=jnp.int32) * blk
    blk_exp = jnp.minimum(jnp.sum(pend[None, :] <= blk_start[:, None], axis=1), N_EXPERTS - 1).astype(jnp.int32)
    slot = jnp.arange(n_blk * blk, dtype=jnp.int32)
    slot_e = jnp.repeat(blk_exp, blk)
    off = slot - pstart[slot_e]
    src = order[jnp.clip(cstart[slot_e] + off, 0, n_assign - 1)] // 2
    slot_ok = (off < counts[slot_e]) & (slot < pend[-1])
    slot_tok = jnp.where(slot_ok, src, 0)
    xb = x1[slot_tok]
    n_used = (pend[-1] // blk).astype(jnp.int32)[None]
    yb = _moe_experts(xb, blk_exp, n_used, *w_experts, layer, blk)
    dest2 = dest.reshape(t, 2)
    gates = jnp.pad(gate, ((0, 0), (0, LANES - 2)))
    return _combine(x1, yb[dest2[:, 0]], yb[dest2[:, 1]], gates, lw)


def _trunk_layer(x, x_bf, lw, layer, mem, conv_buf, ssm_all, shift_buf, wkv_s, w_experts, bsz, seq):
    tm = 512
    p_z = _proj(x_bf, lw['w_in_z'], tm, 2048)
    p_xbc = _proj(x_bf, lw['w_in_xbc'], tm, 2048)
    p_dt = _proj(x_bf, lw['w_in_dt'], tm, LANES)
    p_rwkv = _proj(x_bf, lw['w_in_rwkv'], tm, R_SHIFT_W // 2)
    p_q = _proj(x_bf, lw['w_in_q'], tm, 1024)
    p_gate = _proj(x_bf, lw['w_in_gate'], tm, 1536)

    y_a, ssm_new = _mamba(p_z, p_xbc, p_dt, conv_buf, ssm_all, layer, lw, bsz, seq)
    conv_new = p_xbc.reshape(bsz, seq, M_CONV_DIM)[:, seq - (M_CONV - 1):]
    r, w, k, v, a, kk, g_b = _rwkv_prep(p_rwkv, shift_buf, lw, bsz, seq)
    y_b, wkv_new = _wkv(r, w, k, v, a, kk, wkv_s, lw, bsz, seq)
    shift_new = p_rwkv.reshape(bsz, seq, R_SHIFT_W)[:, -1]
    y_c = _attend(p_q, mem, layer, bsz, seq)
    x1, logits = _merge(x, y_a, y_b, g_b, y_c, p_gate, lw)
    x2, x2_bf = _hier_moe_ln(x1, logits, lw, w_experts, layer)
    return x2, x2_bf, conv_new, ssm_new, shift_new, wkv_new


def _layer_weights(l, w):
    w_in = w['w_in'][l]
    lw = {}
    lw['w_in_z'] = w_in[:, :OFF_XBC].astype(BF16)
    lw['w_in_xbc'] = w_in[:, OFF_XBC:OFF_DT].astype(BF16)
    lw['w_in_dt'] = jnp.pad(w_in[:, OFF_DT:OFF_RWKV], ((0, 0), (0, LANES - M_HEADS))).astype(BF16)
    lw['w_in_rwkv'] = w_in[:, OFF_RWKV:OFF_Q].astype(BF16)
    lw['w_in_q'] = w_in[:, OFF_Q:OFF_GATE].astype(BF16)
    lw['w_in_gate'] = w_in[:, OFF_GATE:].astype(BF16)
    lw['conv_w'] = w['conv_w'][l]
    lw['conv_b'] = w['conv_b'][l]
    lw['dt_bias_pad'] = jnp.pad(w['dt_bias'][l], (0, LANES - M_HEADS))[None, :]
    lw['a_log_pad'] = jnp.pad(w['a_log'][l], (0, LANES - M_HEADS))[None, :]
    lw['d_skip_t'] = jnp.broadcast_to(jnp.repeat(w['d_skip'][l], M_HEADDIM)[:, None], (M_INNER, LANES))
    lw['m_norm_w'] = w['m_norm_w'][l]
    lw['r_mu'] = w['r_mu'][l]
    lw['r_w0'] = w['r_w0'][l]
    lw['r_a0'] = w['r_a0'][l]
    zeros64 = jnp.zeros((R_LORA_WA // 2, D), F32)
    lw['r_w2_pad'] = jnp.concatenate([w['r_w2'][l], zeros64], axis=0).astype(BF16)
    lw['r_a2_pad'] = jnp.concatenate([zeros64, w['r_a2'][l]], axis=0).astype(BF16)
    lw['r_g2_bf'] = w['r_g2'][l].astype(BF16)
    lw['r_kk'] = w['r_kk'][l]
    lw['r_ka'] = w['r_ka'][l]
    per = LANES // R_HEADS
    chainify = lambda t: jnp.tile(t.reshape(R_HEADS, R_HS).T, (1, per))
    lw['lnx_w_c'] = chainify(w['r_lnx_w'][l])
    lw['lnx_b_c'] = chainify(w['r_lnx_b'][l])
    lw['rk_c'] = chainify(w['r_rk'][l].reshape(-1))
    lw['w_mem_kv_bf'] = w['w_mem_kv'][l].astype(BF16)
    lw['w_mo_bf'] = w['w_mo'][l].astype(BF16)
    lw['w_ro_bf'] = w['w_ro'][l].astype(BF16)
    lw['w_xo_bf'] = w['w_xo'][l].astype(BF16)
    lw['w_o_bf'] = w['w_o'][l].astype(BF16)
    lw['ln1_g'] = w['ln1_g'][l]
    lw['ln1_b'] = w['ln1_b'][l]
    lw['ln2_g'] = w['ln2_g'][l]
    lw['ln2_b'] = w['ln2_b'][l]
    nr = N_EGROUPS + N_EXPERTS
    lw['w_router'] = jnp.pad(jnp.concatenate([w['w_rg'][l], w['w_re'][l]], axis=1), ((0, 0), (0, LANES - nr)))
    lw['b_router'] = jnp.pad(jnp.concatenate([w['b_rg'][l], w['b_re'][l]]), (0, LANES - nr))[None, :]
    return lw


def kernel(x_prompt, x_sample, mem_prompt, state_ssm, state_conv, state_wkv, state_shift, cache_mem_k, cache_mem_v, w_in, conv_w, conv_b, dt_bias, a_log, d_skip, m_norm_w, r_mu, r_w0, r_w2, r_a0, r_a2, r_g2, r_kk, r_ka, r_rk, r_lnx_w, r_lnx_b, w_mem_kv, w_mo, w_ro, w_xo, w_o, ln1_g, ln1_b, w_rg, b_rg, w_re, b_re, w_gate, w_up, w_down, ln2_g, ln2_b):
    w = dict(w_in=w_in, conv_w=conv_w, conv_b=conv_b, dt_bias=dt_bias, a_log=a_log, d_skip=d_skip,
             m_norm_w=m_norm_w, r_mu=r_mu, r_w0=r_w0, r_w2=r_w2, r_a0=r_a0, r_a2=r_a2, r_g2=r_g2, r_kk=r_kk,
             r_ka=r_ka, r_rk=r_rk, r_lnx_w=r_lnx_w, r_lnx_b=r_lnx_b, w_mem_kv=w_mem_kv, w_mo=w_mo, w_ro=w_ro,
             w_xo=w_xo, w_o=w_o, ln1_g=ln1_g, ln1_b=ln1_b, w_rg=w_rg, b_rg=b_rg, w_re=w_re, b_re=b_re,
             w_gate=w_gate, w_up=w_up, w_down=w_down, ln2_g=ln2_g, ln2_b=ln2_b)
    bp, sp, _ = x_prompt.shape
    bs, ss, _ = x_sample.shape
    n_mem = mem_prompt.shape[1]
    xp = x_prompt.reshape(bp * sp, D)
    xs = x_sample.reshape(bs * ss, D)
    mem_bf = mem_prompt.reshape(bp * n_mem, D).astype(BF16)
    xp_bf = xp.astype(BF16)
    xs_bf = xs.astype(BF16)
    w_experts = (w_gate, w_up, w_down)
    outs = {n: [] for n in ('p_ssm', 'p_conv', 'p_wkv', 'p_shift', 'p_mk', 'p_mv', 's_ssm', 's_conv', 's_wkv',
                            's_shift')}
    for l in range(DEPTH):
        lw = _layer_weights(l, w)
        mkv = _proj(mem_bf, lw['w_mem_kv_bf'], 512, 2048)
        xp, xp_bf, cp, hp, shp, wp = _trunk_layer(xp, xp_bf, lw, l, mkv, None, None, None, None, w_experts, bp, sp)
        xs, xs_bf, cs, hs, shs, ws = _trunk_layer(xs, xs_bf, lw, l, (cache_mem_k, cache_mem_v), state_conv[l],
                                                  state_ssm, state_shift[l], state_wkv[l], w_experts, bs, ss)
        outs['p_ssm'].append(hp)
        outs['p_conv'].append(cp)
        outs['p_wkv'].append(wp)
        outs['p_shift'].append(shp)
        outs['p_mk'].append(mkv[:, :D].reshape(bp, n_mem, X_HEADS, X_HEADDIM))
        outs['p_mv'].append(mkv[:, D:].reshape(bp, n_mem, X_HEADS, X_HEADDIM))
        outs['s_ssm'].append(hs)
        outs['s_conv'].append(cs)
        outs['s_wkv'].append(ws)
        outs['s_shift'].append(shs)
    st = lambda n: jnp.stack(outs[n])
    return (xp.reshape(bp, sp, D), xs.reshape(bs, ss, D), st('p_ssm'), st('p_conv'), st('p_wkv'), st('p_shift'),
            st('p_mk'), st('p_mv'), st('s_ssm'), st('s_conv'), st('s_wkv'), st('s_shift'))
```

```python
import functools

import jax
import jax.numpy as jnp
from jax import lax
from jax.experimental import pallas as pl
from jax.experimental.pallas import tpu as pltpu

F32 = jnp.float32
BF16 = jnp.bfloat16
HIGHEST = lax.Precision.HIGHEST

LANES = 128
SUBLANES = 8
VMEM_LIMIT = 56 * 1024 * 1024

D = 1024
DEPTH = 4
M_INNER = 2048
M_HEADDIM = 64
M_HEADS = 32
M_GROUPS = 8
M_HPG = M_HEADS // M_GROUPS
M_STATE = 128
M_CONV = 4
M_CONV_DIM = M_INNER + 2 * M_GROUPS * M_STATE
M_CHUNK = 128
RMS_EPS = 1e-5
R_HEADS = 16
R_HS = 64
R_LORA_WA = 128
R_LORA_G = 128
R_SHIFT_W = 3 * D + R_LORA_WA + R_LORA_G
R_GN_EPS = 64e-5
MEM_LEN = 256
X_HEADS = 4
X_HEADDIM = 256
OFF_XBC = M_INNER
OFF_DT = OFF_XBC + M_CONV_DIM
OFF_RWKV = OFF_DT + M_HEADS
OFF_Q = OFF_RWKV + R_SHIFT_W
OFF_GATE = OFF_Q + D
N_EGROUPS = 4
E_PER_GROUP = 8
N_EXPERTS = 32
D_EXPERT = 512
LN_EPS = 1e-5
ALPHA = (2 * DEPTH) ** 0.25


def _params(*sem):
    return pltpu.CompilerParams(dimension_semantics=sem, vmem_limit_bytes=VMEM_LIMIT)


def _sigmoid(x):
    return 1.0 / (1.0 + jnp.exp(-x))


def _silu(x):
    return x * _sigmoid(x)


def _softplus(x):
    return jnp.maximum(x, 0.0) + jnp.log1p(jnp.exp(-jnp.abs(x)))


def _layer_norm(h, g, b):
    mu = jnp.mean(h, axis=-1, keepdims=True)
    d = h - mu
    var = jnp.mean(d * d, axis=-1, keepdims=True)
    return d * lax.rsqrt(var + LN_EPS) * g + b


def _proj_kernel(x_ref, w_ref, o_ref):
    o_ref[...] = jnp.dot(x_ref[...], w_ref[...], preferred_element_type=F32)


def _proj(x, w, tm, tn):
    m, k = x.shape
    n = w.shape[1]
    tm = min(tm, m)
    tn = min(tn, n)
    return pl.pallas_call(
        _proj_kernel,
        grid=(n // tn, m // tm),
        in_specs=[pl.BlockSpec((tm, k), lambda j, i: (i, 0)),
                  pl.BlockSpec((k, tn), lambda j, i: (0, j))],
        out_specs=pl.BlockSpec((tm, tn), lambda j, i: (i, j)),
        out_shape=jax.ShapeDtypeStruct((m, n), F32),
        compiler_params=_params("arbitrary", "arbitrary"),
        name="proj",
    )(x, w)


def _mamba_kernel(z_ref, xbc_ref, dt_ref, cbuf_ref, h0_ref, convw_ref, convb_ref, dtb_ref, alog_ref,
                  dskip_ref, mnw_ref, y_ref, hT_ref, ubuf, h_scr, yT_scr, *, rows, nc):
    q = M_CHUNK
    c = pl.program_id(1)

    @pl.when(c == 0)
    def _():
        if rows < q:
            ubuf[...] = jnp.zeros(ubuf.shape, F32)
        ubuf[0:SUBLANES, :] = cbuf_ref[0]
        h_scr[...] = h0_ref[0, 0]

    ubuf[SUBLANES:SUBLANES + rows, :] = xbc_ref[...]
    cw = convw_ref[...]
    acc = ubuf[5:5 + q, :] * cw[0:1]
    acc = acc + ubuf[6:6 + q, :] * cw[1:2]
    acc = acc + ubuf[7:7 + q, :] * cw[2:3]
    acc = acc + ubuf[8:8 + q, :] * cw[3:4]
    acc = acc + convb_ref[...]
    if nc > 1:
        ubuf[0:SUBLANES, :] = ubuf[q:q + SUBLANES, :]
    act = _silu(acc)

    row_id = lax.broadcasted_iota(jnp.int32, (q, q), 0)
    col_id = lax.broadcasted_iota(jnp.int32, (q, q), 1)
    dt_raw = dt_ref[...]
    if rows < q:
        dt_raw = jnp.concatenate([dt_raw, jnp.zeros((q - rows, LANES), F32)], axis=0)
    dt = _softplus(dt_raw + dtb_ref[...])
    a_neg = -jnp.exp(alog_ref[...])
    adt = a_neg * dt
    if rows < q:
        valid = lax.broadcasted_iota(jnp.int32, (q, 1), 0) < rows
        act = jnp.where(valid, act, 0.0)
        adt = jnp.where(valid, adt, 0.0)
    tril = (row_id >= col_id).astype(F32)
    acs = jnp.dot(tril, adt, precision=HIGHEST, preferred_element_type=F32)
    acs_t = acs.T
    dt_t = dt.T
    xs_t = act[:, :M_INNER].T
    upper = row_id <= col_id

    for g in range(M_GROUPS):
        b_g = act[:, M_INNER + g * M_STATE:M_INNER + (g + 1) * M_STATE]
        c_g = act[:, M_INNER + (M_GROUPS + g) * M_STATE:M_INNER + (M_GROUPS + g + 1) * M_STATE]
        b_bf = b_g.astype(BF16)
        c_t = c_g.T
        c_t_bf = c_t.astype(BF16)
        cb_t = jnp.dot(b_bf, c_t_bf, preferred_element_type=F32)
        for r in range(M_HPG):
            h = g * M_HPG + r
            a_row = acs_t[h:h + 1, :]
            a_col = acs[:, h:h + 1]
            lm_t = jnp.exp(jnp.where(upper, a_row - a_col, -jnp.inf))
            m_t = (cb_t * lm_t).astype(BF16)
            cs_t = (c_t * jnp.exp(a_row)).astype(BF16)
            x_t = xs_t[h * M_HEADDIM:(h + 1) * M_HEADDIM, :]
            xdt_t = x_t * dt_t[h:h + 1, :]
            hprev = h_scr[h]
            y_t = (jnp.dot(xdt_t.astype(BF16), m_t, preferred_element_type=F32)
                   + jnp.dot(hprev.astype(BF16), cs_t, preferred_element_type=F32))
            yT_scr[h * M_HEADDIM:(h + 1) * M_HEADDIM, :] = y_t + x_t * dskip_ref[h * M_HEADDIM:(h + 1) * M_HEADDIM, :]
            a_last = acs[q - 1:q, h:h + 1]
            wrow = jnp.exp(a_last - a_row)
            s_c = jnp.dot((xdt_t * wrow).astype(BF16), b_bf, preferred_element_type=F32)
            h_scr[h] = hprev * jnp.exp(a_last) + s_c

    y = yT_scr[...].T
    if rows < q:
        y = y[0:rows, :]
    y = y * _silu(z_ref[...])
    gw = M_INNER // M_GROUPS
    parts = []
    for g in range(M_GROUPS):
        yg = y[:, g * gw:(g + 1) * gw]
        ms = jnp.mean(yg * yg, axis=-1, keepdims=True)
        parts.append(yg * lax.rsqrt(ms + RMS_EPS))
    y_ref[...] = jnp.concatenate(parts, axis=-1) * mnw_ref[...]

    @pl.when(c == nc - 1)
    def _():
        hT_ref[0] = h_scr[...]


def _mamba(p_z, p_xbc, p_dt, conv_buf, h0_all, layer, lw, bsz, seq):
    q = M_CHUNK
    if seq >= q:
        rows, nc = q, seq // q
    else:
        rows, nc = seq, 1
    if conv_buf is None:
        cbuf = jnp.zeros((1, SUBLANES, M_CONV_DIM), F32)
        cbuf_map = lambda b, c: (0, 0, 0)
    else:
        cbuf = jnp.pad(conv_buf, ((0, 0), (SUBLANES - (M_CONV - 1), 0), (0, 0)))
        cbuf_map = lambda b, c: (b, 0, 0)
    if h0_all is None:
        h0_all = jnp.zeros((1, 1, M_HEADS, M_HEADDIM, M_STATE), F32)
        h0_map = lambda b, c: (0, 0, 0, 0, 0)
    else:
        h0_map = lambda b, c: (layer, b, 0, 0, 0)
    convw = jnp.pad(lw['conv_w'], ((0, SUBLANES - M_CONV), (0, 0)))
    kern = functools.partial(_mamba_kernel, rows=rows, nc=nc)
    row_map = lambda b, c: (b * nc + c, 0)
    const2 = lambda b, c: (0, 0)
    y, h_last = pl.pallas_call(
        kern,
        grid=(bsz, nc),
        in_specs=[pl.BlockSpec((rows, M_INNER), row_map),
                  pl.BlockSpec((rows, M_CONV_DIM), row_map),
                  pl.BlockSpec((rows, LANES), row_map),
                  pl.BlockSpec((1, SUBLANES, M_CONV_DIM), cbuf_map),
                  pl.BlockSpec((1, 1, M_HEADS, M_HEADDIM, M_STATE), h0_map),
                  pl.BlockSpec((SUBLANES, M_CONV_DIM), const2),
                  pl.BlockSpec((1, M_CONV_DIM), const2),
                  pl.BlockSpec((1, LANES), const2),
                  pl.BlockSpec((1, LANES), const2),
                  pl.BlockSpec((M_INNER, LANES), const2),
                  pl.BlockSpec((1, M_INNER), const2)],
        out_specs=[pl.BlockSpec((rows, M_INNER), row_map),
                   pl.BlockSpec((1, M_HEADS, M_HEADDIM, M_STATE), lambda b, c: (b, 0, 0, 0))],
        out_shape=[jax.ShapeDtypeStruct((bsz * seq, M_INNER), F32),
                   jax.ShapeDtypeStruct((bsz, M_HEADS, M_HEADDIM, M_STATE), F32)],
        scratch_shapes=[pltpu.VMEM((q + 2 * SUBLANES, M_CONV_DIM), F32),
                        pltpu.VMEM((M_HEADS, M_HEADDIM, M_STATE), F32),
                        pltpu.VMEM((M_INNER, q), F32)],
        compiler_params=_params("arbitrary", "arbitrary"),
        name="mamba_ssd",
    )(p_z, p_xbc, p_dt, cbuf, h0_all, convw, lw['conv_b'][None, :], lw['dt_bias_pad'], lw['a_log_pad'],
      lw['d_skip_t'], lw['m_norm_w'][None, :])
    return y, h_last


def _rwkv_prep_kernel(cols_ref, shift_ref, mu_ref, w0_ref, w2_ref, a0_ref, a2_ref, g2_ref, kkw_ref, kaw_ref,
                      r_out, w_out, k_out, v_out, a_out, kk_out, g_out, sbuf, *, tl, nt):
    t = pl.program_id(1)

    @pl.when(t == 0)
    def _():
        sbuf[0:SUBLANES, :] = shift_ref[0]

    cols = cols_ref[...]
    sbuf[SUBLANES:SUBLANES + tl, :] = cols
    prev = sbuf[SUBLANES - 1:SUBLANES - 1 + tl, :]
    if nt > 1:
        sbuf[0:SUBLANES, :] = sbuf[tl:tl + SUBLANES, :]
    mixed = cols + (prev - cols) * mu_ref[...]
    r = mixed[:, 0:D]
    k = mixed[:, D:2 * D]
    v = mixed[:, 2 * D:3 * D]
    wa = mixed[:, 3 * D:3 * D + R_LORA_WA]
    gl = mixed[:, 3 * D + R_LORA_WA:]
    lw_ = jnp.dot(jnp.tanh(wa).astype(BF16), w2_ref[...], preferred_element_type=F32)
    la_ = jnp.dot(wa.astype(BF16), a2_ref[...], preferred_element_type=F32)
    g = jnp.dot(_sigmoid(gl).astype(BF16), g2_ref[...], preferred_element_type=F32)
    w_log = -_softplus(-(w0_ref[...] + lw_)) - 0.5
    decay = jnp.exp(-jnp.exp(w_log))
    a = _sigmoid(a0_ref[...] + la_)
    r_out[...] = r
    w_out[...] = decay
    k_out[...] = k * (1.0 + (a - 1.0) * kaw_ref[...])
    v_out[...] = v
    a_out[...] = a
    kk_out[...] = k * kkw_ref[...]
    g_out[...] = g


def _rwkv_prep(p_rwkv, shift_buf, lw, bsz, seq):
    tl = min(seq, 256)
    nt = seq // tl
    if shift_buf is None:
        sb = jnp.zeros((1, SUBLANES, R_SHIFT_W), F32)
        sb_map = lambda b, t: (0, 0, 0)
    else:
        sb = jnp.pad(shift_buf[:, None, :], ((0, 0), (SUBLANES - 1, 0), (0, 0)))
        sb_map = lambda b, t: (b, 0, 0)
    row_map = lambda b, t: (b * nt + t, 0)
    const2 = lambda b, t: (0, 0)
    vec = pl.BlockSpec((1, D), const2)
    outs = pl.pallas_call(
        functools.partial(_rwkv_prep_kernel, tl=tl, nt=nt),
        grid=(bsz, nt),
        in_specs=[pl.BlockSpec((tl, R_SHIFT_W), row_map),
                  pl.BlockSpec((1, SUBLANES, R_SHIFT_W), sb_map),
                  pl.BlockSpec((1, R_SHIFT_W), const2),
                  vec, pl.BlockSpec((R_LORA_WA, D), const2),
                  vec, pl.BlockSpec((R_LORA_WA, D), const2),
                  pl.BlockSpec((R_LORA_G, D), const2),
                  vec, vec],
        out_specs=[pl.BlockSpec((tl, D), row_map)] * 7,
        out_shape=[jax.ShapeDtypeStruct((bsz * seq, D), F32)] * 7,
        scratch_shapes=[pltpu.VMEM((tl + 2 * SUBLANES, R_SHIFT_W), F32)],
        compiler_params=_params("arbitrary", "arbitrary"),
        name="rwkv_prep",
    )(p_rwkv, sb, lw['r_mu'][None, :], lw['r_w0'][None, :], lw['r_w2_pad'], lw['r_a0'][None, :], lw['r_a2_pad'],
      lw['r_g2_bf'], lw['r_kk'][None, :], lw['r_ka'][None, :])
    return outs


CH_B = LANES // R_HEADS
CH_HALF = LANES // 2


def _wkv_kernel(r_ref, w_ref, k_ref, v_ref, a_ref, kk_ref, s0_ref, lnw_ref, lnb_ref, rk_ref,
                y_ref, sT_ref, s_scr, ch_scr, m_scr, tok_scr, yflat_scr, *, tb, nb):
    ng = R_HS // SUBLANES
    nhh = R_HEADS // 2
    tblk = pl.program_id(1)
    low_half = lax.broadcasted_iota(jnp.int32, (SUBLANES, LANES), 1) < CH_HALF

    @pl.when(tblk == 0)
    def _():
        s_scr[...] = s0_ref[0]

    sub_id = lax.broadcasted_iota(jnp.int32, (SUBLANES, LANES), 0)

    def swap_rows(vs):
        for s in (1, 2, 4):
            keep_low = (sub_id & s) == 0
            nxt_vs = []
            for i in range(SUBLANES):
                other = vs[i ^ s]
                if i & s:
                    nxt_vs.append(jnp.where(keep_low, pltpu.roll(other, SUBLANES - s, axis=0), vs[i]))
                else:
                    nxt_vs.append(jnp.where(keep_low, vs[i], pltpu.roll(other, s, axis=0)))
            vs = nxt_vs
        return vs

    for idx, ref in enumerate((r_ref, w_ref, k_ref, v_ref, a_ref, kk_ref)):
        def regroup(hh, carry, idx=idx, ref=ref):
            lanes = pl.ds(pl.multiple_of(hh * LANES, LANES), LANES)
            for tg in range(tb // SUBLANES):
                tiles = swap_rows([ref[b, tg * SUBLANES:(tg + 1) * SUBLANES, lanes] for b in range(CH_B)])
                for tt in range(SUBLANES):
                    tok_scr[idx, hh, tg * SUBLANES + tt] = tiles[tt]
            return carry

        lax.fori_loop(0, nhh, regroup, 0)

    def rows_at(idx, t, hh):
        return tok_scr[idx, hh, t]

    def stage_rows(t):
        for p in range(3):
            for hh in range(nhh):
                av, bv = rows_at(2 * p, t, hh), rows_at(2 * p + 1, t, hh)
                m_scr[p, hh * CH_B:(hh + 1) * CH_B, :] = jnp.where(low_half, av, pltpu.roll(bv, CH_HALF, axis=1))
                m_scr[p, CH_HALF + hh * CH_B:CH_HALF + (hh + 1) * CH_B, :] = jnp.where(
                    low_half, pltpu.roll(av, CH_HALF, axis=1), bv)

    def stage_chain(slot):
        rw, kv, akk = m_scr[0].T, m_scr[1].T, m_scr[2].T
        kkr = akk[R_HS:]
        nrm = jnp.sqrt(jnp.sum(kkr * kkr, axis=0, keepdims=True))
        kk = kkr / jnp.maximum(nrm, 1e-12)
        for idx, val in enumerate((rw[:R_HS], rw[R_HS:], kv[:R_HS], kv[R_HS:], kk * akk[:R_HS], -kk)):
            ch_scr[slot, idx] = val

    def row(ref_view, j):
        return jnp.broadcast_to(ref_view[pl.ds(j, 1), :], (SUBLANES, LANES))

    zeros = tuple(jnp.zeros((SUBLANES, LANES), F32) for _ in range(ng))
    stage_rows(0)
    stage_chain(0)
    stage_rows(min(1, tb - 1))
    stage_chain(1)
    stage_rows(min(2, tb - 1))

    def first_sa(j, acc):
        aj = row(ch_scr.at[0, 5], j)
        return tuple(acc[ig] + s_scr[ig, j] * aj for ig in range(ng))

    sa0 = lax.fori_loop(0, R_HS, first_sa, zeros)

    def step(t, sa):
        cur = t % 3
        nxt = (t + 1) % 3
        v = ch_scr[cur, 3]
        vs = tuple(v[ig * SUBLANES:(ig + 1) * SUBLANES, :] for ig in range(ng))
        r_t, w_t, k_t, b_t, a_next = (ch_scr.at[cur, 0], ch_scr.at[cur, 1], ch_scr.at[cur, 2], ch_scr.at[cur, 4],
                                      ch_scr.at[nxt, 5])

        def col(j, carry):
            yacc, san = carry
            wj, bj, kj, rj, aj = row(w_t, j), row(b_t, j), row(k_t, j), row(r_t, j), row(a_next, j)
            ynew, snew = [], []
            for ig in range(ng):
                s = s_scr[ig, j] * wj + sa[ig] * bj + vs[ig] * kj
                s_scr[ig, j] = s
                ynew.append(yacc[ig] + s * rj)
                snew.append(san[ig] + s * aj)
            return tuple(ynew), tuple(snew)

        yacc, san = lax.fori_loop(0, R_HS, col, (zeros, zeros), unroll=8)
        y = jnp.concatenate(yacc, axis=0)
        mu = jnp.mean(y, axis=0, keepdims=True)
        d = y - mu
        var = jnp.mean(d * d, axis=0, keepdims=True)
        yn = d * lax.rsqrt(var + R_GN_EPS) * lnw_ref[...] + lnb_ref[...]
        bonus = jnp.sum(ch_scr[cur, 0] * ch_scr[cur, 2] * rk_ref[...], axis=0, keepdims=True) * v
        yo = yn + bonus
        yt = jnp.concatenate([yo, yo], axis=0).T
        for hh in range(nhh):
            even = yt[hh * CH_B:(hh + 1) * CH_B]
            odd = yt[CH_HALF + hh * CH_B:CH_HALF + (hh + 1) * CH_B]
            yflat_scr[hh, t] = jnp.where(low_half, even, odd)
        stage_chain((t + 2) % 3)
        stage_rows(jnp.minimum(t + 3, tb - 1))
        return tuple(san)

    lax.fori_loop(0, tb, step, sa0)

    def ungroup(hh, carry):
        lanes = pl.ds(pl.multiple_of(hh * LANES, LANES), LANES)
        for tg in range(tb // SUBLANES):
            tiles = swap_rows([yflat_scr[hh, tg * SUBLANES + tt] for tt in range(SUBLANES)])
            for b in range(CH_B):
                y_ref[b, tg * SUBLANES:(tg + 1) * SUBLANES, lanes] = tiles[b]
        return carry

    lax.fori_loop(0, nhh, ungroup, 0)

    @pl.when(tblk == nb - 1)
    def _():
        sT_ref[0] = s_scr[...]


def _chain_const(t):
    th = t.reshape(R_HEADS // 2, 2, R_HS).transpose(2, 1, 0)
    return jnp.broadcast_to(th[..., None], (R_HS, 2, R_HEADS // 2, CH_B)).reshape(R_HS, LANES)


def _wkv(r, w, k, v, a, kk, s0, lw, bsz, seq):
    g = bsz // CH_B
    ng = R_HS // SUBLANES
    nhh = R_HEADS // 2
    tb = min(seq, 32)
    nb = seq // tb
    tok =[t.reshape(bsz, seq, D) for t in (r, w, k, v, a, kk)]
    st_shape = (ng, R_HS, SUBLANES, LANES)
    if s0 is None:
        s0c = jnp.zeros((1,) + st_shape, F32)
        s_in = pl.BlockSpec((1,) + st_shape, lambda gi, ti: (0, 0, 0, 0, 0))
    else:
        s0c = s0.reshape(g, CH_B, nhh, 2, ng, SUBLANES, R_HS).transpose(0, 4, 6, 5, 3, 2, 1).reshape((g,) + st_shape)
        s_in = pl.BlockSpec((1,) + st_shape, lambda gi, ti: (gi, 0, 0, 0, 0))
    blk = pl.BlockSpec((CH_B, tb, D), lambda gi, ti: (gi, ti, 0))
    s_out_spec = pl.BlockSpec((1,) + st_shape, lambda gi, ti: (gi, 0, 0, 0, 0))
    cblk = pl.BlockSpec((R_HS, LANES), lambda gi, ti: (0, 0))
    y, s_last = pl.pallas_call(
        functools.partial(_wkv_kernel, tb=tb, nb=nb),
        grid=(g, nb),
        in_specs=[blk] * 6 + [s_in, cblk, cblk, cblk],
        out_specs=[blk, s_out_spec],
        out_shape=[jax.ShapeDtypeStruct((bsz, seq, D), F32),
                   jax.ShapeDtypeStruct((g,) + st_shape, F32)],
        scratch_shapes=[pltpu.VMEM(st_shape, F32),
                        pltpu.VMEM((3, 6, R_HS, LANES), F32),
                        pltpu.VMEM((3, LANES, LANES), F32),
                        pltpu.VMEM((6, nhh, tb, CH_B, LANES), F32),
                        pltpu.VMEM((nhh, tb, CH_B, LANES), F32)],
        compiler_params=_params("arbitrary", "arbitrary"),
        name="wkv7",
    )(*tok, s0c, lw['lnx_w_c'], lw['lnx_b_c'], lw['rk_c'])
    s_out = s_last.reshape(g, ng, R_HS, SUBLANES, 2, nhh, CH_B).transpose(0, 6, 5, 4, 1, 3, 2)
    return y.reshape(bsz * seq, D), s_out.reshape(bsz, R_HEADS, R_HS, R_HS)


def _attn_kernel(q_ref, k_ref, v_ref, o_ref, *, head_major_cols):
    q = q_ref[...]
    scale = X_HEADDIM ** -0.5
    outs = []
    for h in range(X_HEADS):
        sl = slice(h * X_HEADDIM, (h + 1) * X_HEADDIM)
        qh = q[:, sl].astype(BF16)
        if head_major_cols:
            kh = k_ref[:, sl].astype(BF16)
            vh = v_ref[:, sl].astype(BF16)
        else:
            kh = k_ref[0, 0, :, h, :].astype(BF16)
            vh = v_ref[0, 0, :, h, :].astype(BF16)
        s = lax.dot_general(qh, kh, (((1,), (1,)), ((), ())), preferred_element_type=F32) * scale
        s = s - jnp.max(s, axis=-1, keepdims=True)
        e = jnp.exp(s)
        p = e / jnp.sum(e, axis=-1, keepdims=True)
        outs.append(jnp.dot(p.astype(BF16), vh, preferred_element_type=F32))
    o_ref[...] = jnp.concatenate(outs, axis=-1)


def _attend(p_q, mem, layer, bsz, seq):
    tl = min(seq, 512)
    nt = seq // tl
    if isinstance(mem, tuple):
        cache_k, cache_v = mem
        cblk = pl.BlockSpec((1, 1, MEM_LEN, X_HEADS, X_HEADDIM), lambda b, t: (layer, b, 0, 0, 0))
        kv_specs, kv_args, head_major_cols = [cblk, cblk], (cache_k, cache_v), False
    else:
        kv_specs = [pl.BlockSpec((MEM_LEN, D), lambda b, t: (b, 0)), pl.BlockSpec((MEM_LEN, D), lambda b, t: (b, 1))]
        kv_args, head_major_cols = (mem, mem), True
    return pl.pallas_call(
        functools.partial(_attn_kernel, head_major_cols=head_major_cols),
        grid=(bsz, nt),
        in_specs=[pl.BlockSpec((tl, D), lambda b, t: (b * nt + t, 0))] + kv_specs,
        out_specs=pl.BlockSpec((tl, D), lambda b, t: (b * nt + t, 0)),
        out_shape=jax.ShapeDtypeStruct((bsz * seq, D), F32),
        compiler_params=_params("arbitrary", "arbitrary"),
        name="mem_attn",
    )(p_q, *kv_args)


def _merge_kernel(x_ref, ya_ref, yb_ref, g_ref, yc_ref, gate_ref, wmo_ref, wro_ref, wxo_ref, wo_ref,
                  lng_ref, lnb_ref, wr_ref, br_ref, x1_ref, lg_ref):
    gate = gate_ref[...]
    ma = jnp.dot(ya_ref[...].astype(BF16), wmo_ref[...], preferred_element_type=F32)
    mb = jnp.dot((yb_ref[...] * g_ref[...]).astype(BF16), wro_ref[...], preferred_element_type=F32)
    mc = jnp.dot(yc_ref[...].astype(BF16), wxo_ref[...], preferred_element_type=F32)
    merged = (_sigmoid(gate[:, 0:D]) * ma + _sigmoid(gate[:, D:2 * D]) * mb) + _sigmoid(gate[:, 2 * D:3 * D]) * mc
    h = ALPHA * x_ref[...] + jnp.dot(merged.astype(BF16), wo_ref[...], preferred_element_type=F32)
    x1 = _layer_norm(h, lng_ref[...], lnb_ref[...])
    x1_ref[...] = x1
    lg_ref[...] = jnp.dot(x1, wr_ref[...], precision=HIGHEST, preferred_element_type=F32) + br_ref[...]


def _merge(x, y_a, y_b, g_b, y_c, p_gate, lw):
    m = x.shape[0]
    tm = min(m, 256)
    row = lambda w: pl.BlockSpec((tm, w), lambda i: (i, 0))
    full = lambda a, b: pl.BlockSpec((a, b), lambda i: (0, 0))
    return pl.pallas_call(
        _merge_kernel,
        grid=(m // tm,),
        in_specs=[row(D), row(M_INNER), row(D), row(D), row(D), row(3 * D),
                  full(M_INNER, D), full(D, D), full(D, D), full(D, D),
                  full(1, D), full(1, D), full(D, LANES), full(1, LANES)],
        out_specs=[row(D), row(LANES)],
        out_shape=[jax.ShapeDtypeStruct((m, D), F32), jax.ShapeDtypeStruct((m, LANES), F32)],
        compiler_params=_params("arbitrary"),
        name="merge_ln_router",
    )(x, y_a, y_b, g_b, y_c, p_gate, lw['w_mo_bf'], lw['w_ro_bf'], lw['w_xo_bf'], lw['w_o_bf'],
      lw['ln1_g'][None, :], lw['ln1_b'][None, :], lw['w_router'], lw['b_router'])


def _moe_kernel(be_ref, nu_ref, x_ref, wg_ref, wu_ref, wd_ref, o_ref, wg_bf, wu_bf, wd_bf):
    i = pl.program_id(0)
    active = i < nu_ref[0]
    new_expert = jnp.logical_or(i == 0, be_ref[i] != be_ref[jnp.maximum(i - 1, 0)])

    @pl.when(jnp.logical_and(active, new_expert))
    def _():
        wg_bf[...] = wg_ref[0, 0].astype(BF16)
        wu_bf[...] = wu_ref[0, 0].astype(BF16)
        wd_bf[...] = wd_ref[0, 0].astype(BF16)

    @pl.when(active)
    def _():
        xb = x_ref[...].astype(BF16)
        hid = _silu(jnp.dot(xb, wg_bf[...], preferred_element_type=F32)) * jnp.dot(xb, wu_bf[...],
                                                                                    preferred_element_type=F32)
        o_ref[...] = jnp.dot(hid.astype(BF16), wd_bf[...], preferred_element_type=F32)

    @pl.when(jnp.logical_not(active))
    def _():
        o_ref[...] = jnp.zeros(o_ref.shape, F32)


def _moe_experts(xb, blk_exp, n_used, w_gate, w_up, w_down, layer, blk):
    n_blk = xb.shape[0] // blk
    grid_spec = pltpu.PrefetchScalarGridSpec(
        num_scalar_prefetch=2,
        grid=(n_blk,),
        in_specs=[pl.BlockSpec((blk, D), lambda i, be, nu: (i, 0)),
                  pl.BlockSpec((1, 1, D, D_EXPERT), lambda i, be, nu: (layer, be[i], 0, 0)),
                  pl.BlockSpec((1, 1, D, D_EXPERT), lambda i, be, nu: (layer, be[i], 0, 0)),
                  pl.BlockSpec((1, 1, D_EXPERT, D), lambda i, be, nu: (layer, be[i], 0, 0))],
        out_specs=pl.BlockSpec((blk, D), lambda i, be, nu: (i, 0)),
        scratch_shapes=[pltpu.VMEM((D, D_EXPERT), BF16), pltpu.VMEM((D, D_EXPERT), BF16),
                        pltpu.VMEM((D_EXPERT, D), BF16)],
    )
    return pl.pallas_call(
        _moe_kernel,
        grid_spec=grid_spec,
        out_shape=jax.ShapeDtypeStruct((n_blk * blk, D), F32),
        compiler_params=_params("arbitrary"),
        name="moe_experts",
    )(blk_exp, n_used, xb, w_gate, w_up, w_down)


def _combine_kernel(x_ref, y0_ref, y1_ref, gt_ref, lng_ref, lnb_ref, o_ref, ob_ref):
    gt = gt_ref[...]
    moe = gt[:, 0:1] * y0_ref[...] + gt[:, 1:2] * y1_ref[...]
    x2 = _layer_norm(ALPHA * x_ref[...] + moe, lng_ref[...], lnb_ref[...])
    o_ref[...] = x2
    ob_ref[...] = x2.astype(BF16)


def _combine(x1, y0, y1, gates, lw):
    m = x1.shape[0]
    tm = min(m, 512)
    row = lambda w: pl.BlockSpec((tm, w), lambda i: (i, 0))
    full = lambda a, b: pl.BlockSpec((a, b), lambda i: (0, 0))
    return pl.pallas_call(
        _combine_kernel,
        grid=(m // tm,),
        in_specs=[row(D), row(D), row(D), row(LANES), full(1, D), full(1, D)],
        out_specs=[row(D), row(D)],
        out_shape=[jax.ShapeDtypeStruct((m, D), F32), jax.ShapeDtypeStruct((m, D), BF16)],
        compiler_params=_params("arbitrary"),
        name="moe_combine_ln",
    )(x1, y0, y1, gates, lw['ln2_g'][None, :], lw['ln2_b'][None, :])


def _hier_moe_ln(x1, logits, lw, w_experts, layer):
    t = x1.shape[0]
    blk = 256 if t >= 8192 else 128
    lg = logits[:, :N_EGROUPS]
    le = logits[:, N_EGROUPS:N_EGROUPS + N_EXPERTS].reshape(t, N_EGROUPS, E_PER_GROUP)
    g_sel = jnp.argmax(lg, axis=-1).astype(jnp.int32)
    g_prob = jnp.take_along_axis(jax.nn.softmax(lg, axis=-1), g_sel[:, None], axis=-1)
    le = jnp.take_along_axis(le, g_sel[:, None, None], axis=1)[:, 0]
    top_v, top_i = lax.top_k(le, 2)
    gate = g_prob * jax.nn.softmax(top_v, axis=-1)
    flat_e = (g_sel[:, None] * E_PER_GROUP + top_i.astype(jnp.int32)).reshape(-1)
    n_assign = 2 * t
    order = jnp.argsort(flat_e).astype(jnp.int32)
    onehot = (flat_e[:, None] == jnp.arange(N_EXPERTS, dtype=jnp.int32)[None, :]).astype(jnp.int32)
    seen = jnp.cumsum(onehot, axis=0)
    counts = seen[-1]
    rank = jnp.sum(seen * onehot, axis=1) - 1
    padded = (counts + blk - 1) // blk * blk
    pend = jnp.cumsum(padded)
    pstart = pend - padded
    cstart = jnp.cumsum(counts) - counts
    dest = pstart[flat_e] + rank
    n_blk = n_assign // blk + N_EXPERTS
    blk_start = jnp.arange(n_blk, dtype=jnp.int32) * blk
    blk_exp = jnp.minimum(jnp.sum(pend[None, :] <= blk_start[:, None], axis=1), N_EXPERTS - 1).astype(jnp.int32)
    slot = jnp.arange(n_blk * blk, dtype=jnp.int32)
    slot_e = jnp.repeat(blk_exp, blk)
    off = slot - pstart[slot_e]
    src = order[jnp.clip(cstart[slot_e] + off, 0, n_assign - 1)] // 2
    slot_ok = (off < counts[slot_e]) & (slot < pend[-1])
    slot_tok = jnp.where(slot_ok, src, 0)
    xb = x1[slot_tok]
    n_used = (pend[-1] // blk).astype(jnp.int32)[None]
    yb = _moe_experts(xb, blk_exp, n_used, *w_experts, layer, blk)
    dest2 = dest.reshape(t, 2)
    gates = jnp.pad(gate, ((0, 0), (0, LANES - 2)))
    return _combine(x1, yb[dest2[:, 0]], yb[dest2[:, 1]], gates, lw)


def _trunk_layer(x, x_bf, lw, layer, mem, conv_buf, ssm_all, shift_buf, wkv_s, w_experts, bsz, seq):
    tm = 512
    p_z = _proj(x_bf, lw['w_in_z'], tm, 2048)
    p_xbc = _proj(x_bf, lw['w_in_xbc'], tm, 2048)
    p_dt = _proj(x_bf, lw['w_in_dt'], tm, LANES)
    p_rwkv = _proj(x_bf, lw['w_in_rwkv'], tm, R_SHIFT_W // 2)
    p_q = _proj(x_bf, lw['w_in_q'], tm, 1024)
    p_gate = _proj(x_bf, lw['w_in_gate'], tm, 1536)

    y_a, ssm_new = _mamba(p_z, p_xbc, p_dt, conv_buf, ssm_all, layer, lw, bsz, seq)
    conv_new = p_xbc.reshape(bsz, seq, M_CONV_DIM)[:, seq - (M_CONV - 1):]
    r, w, k, v, a, kk, g_b = _rwkv_prep(p_rwkv, shift_buf, lw, bsz, seq)
    y_b, wkv_new = _wkv(r, w, k, v, a, kk, wkv_s, lw, bsz, seq)
    shift_new = p_rwkv.reshape(bsz, seq, R_SHIFT_W)[:, -1]
    y_c = _attend(p_q, mem, layer, bsz, seq)
    x1, logits = _merge(x, y_a, y_b, g_b, y_c, p_gate, lw)
    x2, x2_bf = _hier_moe_ln(x1, logits, lw, w_experts, layer)
    return x2, x2_bf, conv_new, ssm_new, shift_new, wkv_new


def _layer_weights(l, w):
    w_in = w['w_in'][l]
    lw = {}
    lw['w_in_z'] = w_in[:, :OFF_XBC].astype(BF16)
    lw['w_in_xbc'] = w_in[:, OFF_XBC:OFF_DT].astype(BF16)
    lw['w_in_dt'] = jnp.pad(w_in[:, OFF_DT:OFF_RWKV], ((0, 0), (0, LANES - M_HEADS))).astype(BF16)
    lw['w_in_rwkv'] = w_in[:, OFF_RWKV:OFF_Q].astype(BF16)
    lw['w_in_q'] = w_in[:, OFF_Q:OFF_GATE].astype(BF16)
    lw['w_in_gate'] = w_in[:, OFF_GATE:].astype(BF16)
    lw['conv_w'] = w['conv_w'][l]
    lw['conv_b'] = w['conv_b'][l]
    lw['dt_bias_pad'] = jnp.pad(w['dt_bias'][l], (0, LANES - M_HEADS))[None, :]
    lw['a_log_pad'] = jnp.pad(w['a_log'][l], (0, LANES - M_HEADS))[None, :]
    lw['d_skip_t'] = jnp.broadcast_to(jnp.repeat(w['d_skip'][l], M_HEADDIM)[:, None], (M_INNER, LANES))
    lw['m_norm_w'] = w['m_norm_w'][l]
    lw['r_mu'] = w['r_mu'][l]
    lw['r_w0'] = w['r_w0'][l]
    lw['r_a0'] = w['r_a0'][l]
    zeros64 = jnp.zeros((R_LORA_WA // 2, D), F32)
    lw['r_w2_pad'] = jnp.concatenate([w['r_w2'][l], zeros64], axis=0).astype(BF16)
    lw['r_a2_pad'] = jnp.concatenate([zeros64, w['r_a2'][l]], axis=0).astype(BF16)
    lw['r_g2_bf'] = w['r_g2'][l].astype(BF16)
    lw['r_kk'] = w['r_kk'][l]
    lw['r_ka'] = w['r_ka'][l]
    lw['lnx_w_c'] = _chain_const(w['r_lnx_w'][l])
    lw['lnx_b_c'] = _chain_const(w['r_lnx_b'][l])
    lw['rk_c'] = _chain_const(w['r_rk'][l].reshape(-1))
    lw['w_mem_kv_bf'] = w['w_mem_kv'][l].astype(BF16)
    lw['w_mo_bf'] = w['w_mo'][l].astype(BF16)
    lw['w_ro_bf'] = w['w_ro'][l].astype(BF16)
    lw['w_xo_bf'] = w['w_xo'][l].astype(BF16)
    lw['w_o_bf'] = w['w_o'][l].astype(BF16)
    lw['ln1_g'] = w['ln1_g'][l]
    lw['ln1_b'] = w['ln1_b'][l]
    lw['ln2_g'] = w['ln2_g'][l]
    lw['ln2_b'] = w['ln2_b'][l]
    nr = N_EGROUPS + N_EXPERTS
    lw['w_router'] = jnp.pad(jnp.concatenate([w['w_rg'][l], w['w_re'][l]], axis=1), ((0, 0), (0, LANES - nr)))
    lw['b_router'] = jnp.pad(jnp.concatenate([w['b_rg'][l], w['b_re'][l]]), (0, LANES - nr))[None, :]
    return lw


def kernel(x_prompt, x_sample, mem_prompt, state_ssm, state_conv, state_wkv, state_shift, cache_mem_k, cache_mem_v, w_in, conv_w, conv_b, dt_bias, a_log, d_skip, m_norm_w, r_mu, r_w0, r_w2, r_a0, r_a2, r_g2, r_kk, r_ka, r_rk, r_lnx_w, r_lnx_b, w_mem_kv, w_mo, w_ro, w_xo, w_o, ln1_g, ln1_b, w_rg, b_rg, w_re, b_re, w_gate, w_up, w_down, ln2_g, ln2_b):
    w = dict(w_in=w_in, conv_w=conv_w, conv_b=conv_b, dt_bias=dt_bias, a_log=a_log, d_skip=d_skip,
             m_norm_w=m_norm_w, r_mu=r_mu, r_w0=r_w0, r_w2=r_w2, r_a0=r_a0, r_a2=r_a2, r_g2=r_g2, r_kk=r_kk,
             r_ka=r_ka, r_rk=r_rk, r_lnx_w=r_lnx_w, r_lnx_b=r_lnx_b, w_mem_kv=w_mem_kv, w_mo=w_mo, w_ro=w_ro,
             w_xo=w_xo, w_o=w_o, ln1_g=ln1_g, ln1_b=ln1_b, w_rg=w_rg, b_rg=b_rg, w_re=w_re, b_re=b_re,
             w_gate=w_gate, w_up=w_up, w_down=w_down, ln2_g=ln2_g, ln2_b=ln2_b)
    bp, sp, _ = x_prompt.shape
    bs, ss, _ = x_sample.shape
    n_mem = mem_prompt.shape[1]
    xp = x_prompt.reshape(bp * sp, D)
    xs = x_sample.reshape(bs * ss, D)
    mem_bf = mem_prompt.reshape(bp * n_mem, D).astype(BF16)
    xp_bf = xp.astype(BF16)
    xs_bf = xs.astype(BF16)
    w_experts = (w_gate, w_up, w_down)
    outs = {n: [] for n in ('p_ssm', 'p_conv', 'p_wkv', 'p_shift', 'p_mk', 'p_mv', 's_ssm', 's_conv', 's_wkv',
                            's_shift')}
    for l in range(DEPTH):
        lw = _layer_weights(l, w)
        mkv = _proj(mem_bf, lw['w_mem_kv_bf'], 512, 2048)
        xp, xp_bf, cp, hp, shp, wp = _trunk_layer(xp, xp_bf, lw, l, mkv, None, None, None, None, w_experts, bp, sp)
        xs, xs_bf, cs, hs, shs, ws = _trunk_layer(xs, xs_bf, lw, l, (cache_mem_k, cache_mem_v), state_conv[l],
                                                  state_ssm, state_shift[l], state_wkv[l], w_experts, bs, ss)
        outs['p_ssm'].append(hp)
        outs['p_conv'].append(cp)
        outs['p_wkv'].append(wp)
        outs['p_shift'].append(shp)
        outs['p_mk'].append(mkv[:, :D].reshape(bp, n_mem, X_HEADS, X_HEADDIM))
        outs['p_mv'].append(mkv[:, D:].reshape(bp, n_mem, X_HEADS, X_HEADDIM))
        outs['s_ssm'].append(hs)
        outs['s_conv'].append(cs)
        outs['s_wkv'].append(ws)
        outs['s_shift'].append(shs)
    st = lambda n: jnp.stack(outs[n])
    return (xp.reshape(bp, sp, D), xs.reshape(bs, ss, D), st('p_ssm'), st('p_conv'), st('p_wkv'), st('p_shift'),
            st('p_mk'), st('p_mv'), st('s_ssm'), st('s_conv'), st('s_wkv'), st('s_shift'))
```

```python
import functools

import jax
import jax.numpy as jnp
from jax import lax
from jax.experimental import pallas as pl
from jax.experimental.pallas import tpu as pltpu

F32 = jnp.float32
BF16 = jnp.bfloat16
HIGHEST = lax.Precision.HIGHEST

LANES = 128
SUBLANES = 8
VMEM_LIMIT = 56 * 1024 * 1024

D = 1024
DEPTH = 4
M_INNER = 2048
M_HEADDIM = 64
M_HEADS = 32
M_GROUPS = 8
M_HPG = M_HEADS // M_GROUPS
M_STATE = 128
M_CONV = 4
M_CONV_DIM = M_INNER + 2 * M_GROUPS * M_STATE
M_CHUNK = 128
RMS_EPS = 1e-5
R_HEADS = 16
R_HS = 64
R_LORA_WA = 128
R_LORA_G = 128
R_SHIFT_W = 3 * D + R_LORA_WA + R_LORA_G
R_GN_EPS = 64e-5
MEM_LEN = 256
X_HEADS = 4
X_HEADDIM = 256
OFF_XBC = M_INNER
OFF_DT = OFF_XBC + M_CONV_DIM
OFF_RWKV = OFF_DT + M_HEADS
OFF_Q = OFF_RWKV + R_SHIFT_W
OFF_GATE = OFF_Q + D
N_EGROUPS = 4
E_PER_GROUP = 8
N_EXPERTS = 32
D_EXPERT = 512
LN_EPS = 1e-5
ALPHA = (2 * DEPTH) ** 0.25


def _params(*sem):
    return pltpu.CompilerParams(dimension_semantics=sem, vmem_limit_bytes=VMEM_LIMIT)


def _sigmoid(x):
    return 1.0 / (1.0 + jnp.exp(-x))


def _silu(x):
    return x * _sigmoid(x)


def _softplus(x):
    return jnp.maximum(x, 0.0) + jnp.log1p(jnp.exp(-jnp.abs(x)))


def _layer_norm(h, g, b):
    mu = jnp.mean(h, axis=-1, keepdims=True)
    d = h - mu
    var = jnp.mean(d * d, axis=-1, keepdims=True)
    return d * lax.rsqrt(var + LN_EPS) * g + b


def _proj_kernel(x_ref, w_ref, o_ref):
    o_ref[...] = jnp.dot(x_ref[...], w_ref[...], preferred_element_type=F32)


def _proj(x, w, tm, tn):
    m, k = x.shape
    n = w.shape[1]
    tm = min(tm, m)
    tn = min(tn, n)
    return pl.pallas_call(
        _proj_kernel,
        grid=(n // tn, m // tm),
        in_specs=[pl.BlockSpec((tm, k), lambda j, i: (i, 0)),
                  pl.BlockSpec((k, tn), lambda j, i: (0, j))],
        out_specs=pl.BlockSpec((tm, tn), lambda j, i: (i, j)),
        out_shape=jax.ShapeDtypeStruct((m, n), F32),
        compiler_params=_params("arbitrary", "arbitrary"),
        name="proj",
    )(x, w)


def _mamba_kernel(z_ref, xbc_ref, dt_ref, cbuf_ref, h0_ref, convw_ref, convb_ref, dtb_ref, alog_ref,
                  dskip_ref, mnw_ref, y_ref, hT_ref, ubuf, h_scr, yT_scr, *, rows, nc, q):
    c = pl.program_id(1)

    @pl.when(c == 0)
    def _():
        if rows < q:
            ubuf[...] = jnp.zeros(ubuf.shape, F32)
        ubuf[0:SUBLANES, :] = cbuf_ref[0]
        h_scr[...] = h0_ref[0, 0]

    ubuf[SUBLANES:SUBLANES + rows, :] = xbc_ref[...]
    cw = convw_ref[...]
    acc = ubuf[5:5 + q, :] * cw[0:1]
    acc = acc + ubuf[6:6 + q, :] * cw[1:2]
    acc = acc + ubuf[7:7 + q, :] * cw[2:3]
    acc = acc + ubuf[8:8 + q, :] * cw[3:4]
    acc = acc + convb_ref[...]
    if nc > 1:
        ubuf[0:SUBLANES, :] = ubuf[q:q + SUBLANES, :]
    act = _silu(acc)

    row_id = lax.broadcasted_iota(jnp.int32, (q, q), 0)
    col_id = lax.broadcasted_iota(jnp.int32, (q, q), 1)
    dt_raw = dt_ref[...]
    if rows < q:
        dt_raw = jnp.concatenate([dt_raw, jnp.zeros((q - rows, LANES), F32)], axis=0)
    dt = _softplus(dt_raw + dtb_ref[...])
    a_neg = -jnp.exp(alog_ref[...])
    adt = a_neg * dt
    if rows < q:
        valid = lax.broadcasted_iota(jnp.int32, (q, 1), 0) < rows
        act = jnp.where(valid, act, 0.0)
        adt = jnp.where(valid, adt, 0.0)
    tril = (row_id >= col_id).astype(F32)
    acs = jnp.dot(tril, adt, precision=HIGHEST, preferred_element_type=F32)
    acs_t = acs.T
    dt_t = dt.T
    xs_t = act[:, :M_INNER].T
    upper = row_id <= col_id

    for g in range(M_GROUPS):
        b_g = act[:, M_INNER + g * M_STATE:M_INNER + (g + 1) * M_STATE]
        c_g = act[:, M_INNER + (M_GROUPS + g) * M_STATE:M_INNER + (M_GROUPS + g + 1) * M_STATE]
        b_bf = b_g.astype(BF16)
        c_t = c_g.T
        c_t_bf = c_t.astype(BF16)
        cb_t = jnp.dot(b_bf, c_t_bf, preferred_element_type=F32)
        for r in range(M_HPG):
            h = g * M_HPG + r
            a_row = acs_t[h:h + 1, :]
            a_col = acs[:, h:h + 1]
            lm_t = jnp.exp(jnp.where(upper, a_row - a_col, -jnp.inf))
            m_t = (cb_t * lm_t).astype(BF16)
            cs_t = (c_t * jnp.exp(a_row)).astype(BF16)
            x_t = xs_t[h * M_HEADDIM:(h + 1) * M_HEADDIM, :]
            xdt_t = x_t * dt_t[h:h + 1, :]
            hprev = h_scr[h]
            y_t = (jnp.dot(xdt_t.astype(BF16), m_t, preferred_element_type=F32)
                   + jnp.dot(hprev.astype(BF16), cs_t, preferred_element_type=F32))
            yT_scr[h * M_HEADDIM:(h + 1) * M_HEADDIM, :] = y_t + x_t * dskip_ref[h * M_HEADDIM:(h + 1) * M_HEADDIM, 0:q]
            a_last = acs[q - 1:q, h:h + 1]
            wrow = jnp.exp(a_last - a_row)
            s_c = jnp.dot((xdt_t * wrow).astype(BF16), b_bf, preferred_element_type=F32)
            h_scr[h] = hprev * jnp.exp(a_last) + s_c

    y = yT_scr[...].T
    if rows < q:
        y = y[0:rows, :]
    y = y * _silu(z_ref[...])
    gw = M_INNER // M_GROUPS
    parts = []
    for g in range(M_GROUPS):
        yg = y[:, g * gw:(g + 1) * gw]
        ms = jnp.mean(yg * yg, axis=-1, keepdims=True)
        parts.append(yg * lax.rsqrt(ms + RMS_EPS))
    y_ref[...] = jnp.concatenate(parts, axis=-1) * mnw_ref[...]

    @pl.when(c == nc - 1)
    def _():
        hT_ref[0] = h_scr[...]


def _mamba(p_z, p_xbc, p_dt, conv_buf, h0_all, layer, lw, bsz, seq):
    if seq >= M_CHUNK:
        q = rows = M_CHUNK
        nc = seq // q
    else:
        q = rows = seq
        nc = 1
    if conv_buf is None:
        cbuf = jnp.zeros((1, SUBLANES, M_CONV_DIM), F32)
        cbuf_map = lambda b, c: (0, 0, 0)
    else:
        cbuf = jnp.pad(conv_buf, ((0, 0), (SUBLANES - (M_CONV - 1), 0), (0, 0)))
        cbuf_map = lambda b, c: (b, 0, 0)
    if h0_all is None:
        h0_all = jnp.zeros((1, 1, M_HEADS, M_HEADDIM, M_STATE), F32)
        h0_map = lambda b, c: (0, 0, 0, 0, 0)
    else:
        h0_map = lambda b, c: (layer, b, 0, 0, 0)
    convw = jnp.pad(lw['conv_w'], ((0, SUBLANES - M_CONV), (0, 0)))
    kern = functools.partial(_mamba_kernel, rows=rows, nc=nc, q=q)
    row_map = lambda b, c: (b * nc + c, 0)
    const2 = lambda b, c: (0, 0)
    y, h_last = pl.pallas_call(
        kern,
        grid=(bsz, nc),
        in_specs=[pl.BlockSpec((rows, M_INNER), row_map),
                  pl.BlockSpec((rows, M_CONV_DIM), row_map),
                  pl.BlockSpec((rows, LANES), row_map),
                  pl.BlockSpec((1, SUBLANES, M_CONV_DIM), cbuf_map),
                  pl.BlockSpec((1, 1, M_HEADS, M_HEADDIM, M_STATE), h0_map),
                  pl.BlockSpec((SUBLANES, M_CONV_DIM), const2),
                  pl.BlockSpec((1, M_CONV_DIM), const2),
                  pl.BlockSpec((1, LANES), const2),
                  pl.BlockSpec((1, LANES), const2),
                  pl.BlockSpec((M_INNER, LANES), const2),
                  pl.BlockSpec((1, M_INNER), const2)],
        out_specs=[pl.BlockSpec((rows, M_INNER), row_map),
                   pl.BlockSpec((1, M_HEADS, M_HEADDIM, M_STATE), lambda b, c: (b, 0, 0, 0))],
        out_shape=[jax.ShapeDtypeStruct((bsz * seq, M_INNER), F32),
                   jax.ShapeDtypeStruct((bsz, M_HEADS, M_HEADDIM, M_STATE), F32)],
        scratch_shapes=[pltpu.VMEM((q + 2 * SUBLANES, M_CONV_DIM), F32),
                        pltpu.VMEM((M_HEADS, M_HEADDIM, M_STATE), F32),
                        pltpu.VMEM((M_INNER, q), F32)],
        compiler_params=_params("arbitrary", "arbitrary"),
        name="mamba_ssd",
    )(p_z, p_xbc, p_dt, cbuf, h0_all, convw, lw['conv_b'][None, :], lw['dt_bias_pad'], lw['a_log_pad'],
      lw['d_skip_t'], lw['m_norm_w'][None, :])
    return y, h_last


def _rwkv_prep_kernel(cols_ref, shift_ref, mu_ref, w0_ref, w2_ref, a0_ref, a2_ref, g2_ref, kkw_ref, kaw_ref,
                      r_out, w_out, k_out, v_out, a_out, kk_out, g_out, sbuf, *, tl, nt):
    t = pl.program_id(1)

    @pl.when(t == 0)
    def _():
        sbuf[0:SUBLANES, :] = shift_ref[0]

    cols = cols_ref[...]
    sbuf[SUBLANES:SUBLANES + tl, :] = cols
    prev = sbuf[SUBLANES - 1:SUBLANES - 1 + tl, :]
    if nt > 1:
        sbuf[0:SUBLANES, :] = sbuf[tl:tl + SUBLANES, :]
    mixed = cols + (prev - cols) * mu_ref[...]
    r = mixed[:, 0:D]
    k = mixed[:, D:2 * D]
    v = mixed[:, 2 * D:3 * D]
    wa = mixed[:, 3 * D:3 * D + R_LORA_WA]
    gl = mixed[:, 3 * D + R_LORA_WA:]
    lw_ = jnp.dot(jnp.tanh(wa).astype(BF16), w2_ref[...], preferred_element_type=F32)
    la_ = jnp.dot(wa.astype(BF16), a2_ref[...], preferred_element_type=F32)
    g = jnp.dot(_sigmoid(gl).astype(BF16), g2_ref[...], preferred_element_type=F32)
    w_log = -_softplus(-(w0_ref[...] + lw_)) - 0.5
    decay = jnp.exp(-jnp.exp(w_log))
    a = _sigmoid(a0_ref[...] + la_)
    r_out[...] = r
    w_out[...] = decay
    k_out[...] = k * (1.0 + (a - 1.0) * kaw_ref[...])
    v_out[...] = v
    a_out[...] = a
    kk_out[...] = k * kkw_ref[...]
    g_out[...] = g


def _rwkv_prep(p_rwkv, shift_buf, lw, bsz, seq):
    tl = min(seq, 256)
    nt = seq // tl
    if shift_buf is None:
        sb = jnp.zeros((1, SUBLANES, R_SHIFT_W), F32)
        sb_map = lambda b, t: (0, 0, 0)
    else:
        sb = jnp.pad(shift_buf[:, None, :], ((0, 0), (SUBLANES - 1, 0), (0, 0)))
        sb_map = lambda b, t: (b, 0, 0)
    row_map = lambda b, t: (b * nt + t, 0)
    const2 = lambda b, t: (0, 0)
    vec = pl.BlockSpec((1, D), const2)
    outs = pl.pallas_call(
        functools.partial(_rwkv_prep_kernel, tl=tl, nt=nt),
        grid=(bsz, nt),
        in_specs=[pl.BlockSpec((tl, R_SHIFT_W), row_map),
                  pl.BlockSpec((1, SUBLANES, R_SHIFT_W), sb_map),
                  pl.BlockSpec((1, R_SHIFT_W), const2),
                  vec, pl.BlockSpec((R_LORA_WA, D), const2),
                  vec, pl.BlockSpec((R_LORA_WA, D), const2),
                  pl.BlockSpec((R_LORA_G, D), const2),
                  vec, vec],
        out_specs=[pl.BlockSpec((tl, D), row_map)] * 7,
        out_shape=[jax.ShapeDtypeStruct((bsz * seq, D), F32)] * 7,
        scratch_shapes=[pltpu.VMEM((tl + 2 * SUBLANES, R_SHIFT_W), F32)],
        compiler_params=_params("arbitrary", "arbitrary"),
        name="rwkv_prep",
    )(p_rwkv, sb, lw['r_mu'][None, :], lw['r_w0'][None, :], lw['r_w2_pad'], lw['r_a0'][None, :], lw['r_a2_pad'],
      lw['r_g2_bf'], lw['r_kk'][None, :], lw['r_ka'][None, :])
    return outs


CH_B = LANES // R_HEADS
CH_HALF = LANES // 2


def _wkv_kernel(r_ref, w_ref, k_ref, v_ref, a_ref, kk_ref, s0_ref, lnw_ref, lnb_ref, rk_ref,
                y_ref, sT_ref, s_scr, ch_scr, m_scr, tok_scr, yflat_scr, *, tb, nb):
    ng = R_HS // SUBLANES
    nhh = R_HEADS // 2
    tblk = pl.program_id(1)
    low_half = lax.broadcasted_iota(jnp.int32, (SUBLANES, LANES), 1) < CH_HALF

    @pl.when(tblk == 0)
    def _():
        s_scr[...] = s0_ref[0]

    sub_id = lax.broadcasted_iota(jnp.int32, (SUBLANES, LANES), 0)

    def swap_rows(vs):
        for s in (1, 2, 4):
            keep_low = (sub_id & s) == 0
            nxt_vs = []
            for i in range(SUBLANES):
                other = vs[i ^ s]
                if i & s:
                    nxt_vs.append(jnp.where(keep_low, pltpu.roll(other, SUBLANES - s, axis=0), vs[i]))
                else:
                    nxt_vs.append(jnp.where(keep_low, vs[i], pltpu.roll(other, s, axis=0)))
            vs = nxt_vs
        return vs

    for idx, ref in enumerate((r_ref, w_ref, k_ref, v_ref, a_ref, kk_ref)):
        def regroup(hh, carry, idx=idx, ref=ref):
            lanes = pl.ds(pl.multiple_of(hh * LANES, LANES), LANES)
            for tg in range(tb // SUBLANES):
                tiles = swap_rows([ref[b, tg * SUBLANES:(tg + 1) * SUBLANES, lanes] for b in range(CH_B)])
                for tt in range(SUBLANES):
                    tok_scr[idx, hh, tg * SUBLANES + tt] = tiles[tt]
            return carry

        lax.fori_loop(0, nhh, regroup, 0)

    def rows_at(idx, t, hh):
        return tok_scr[idx, hh, t]

    def stage_rows(t):
        for p in range(3):
            for hh in range(nhh):
                av, bv = rows_at(2 * p, t, hh), rows_at(2 * p + 1, t, hh)
                m_scr[p, hh * CH_B:(hh + 1) * CH_B, :] = jnp.where(low_half, av, pltpu.roll(bv, CH_HALF, axis=1))
                m_scr[p, CH_HALF + hh * CH_B:CH_HALF + (hh + 1) * CH_B, :] = jnp.where(
                    low_half, pltpu.roll(av, CH_HALF, axis=1), bv)

    def stage_chain(slot):
        rw, kv, akk = m_scr[0].T, m_scr[1].T, m_scr[2].T
        kkr = akk[R_HS:]
        nrm = jnp.sqrt(jnp.sum(kkr * kkr, axis=0, keepdims=True))
        kk = kkr / jnp.maximum(nrm, 1e-12)
        for idx, val in enumerate((rw[:R_HS], rw[R_HS:], kv[:R_HS], kv[R_HS:], kk * akk[:R_HS], -kk)):
            ch_scr[slot, idx] = val

    def row(ref_view, j):
        return jnp.broadcast_to(ref_view[pl.ds(j, 1), :], (SUBLANES, LANES))

    zeros = tuple(jnp.zeros((SUBLANES, LANES), F32) for _ in range(ng))
    stage_rows(0)
    stage_chain(0)
    stage_rows(min(1, tb - 1))
    stage_chain(1)
    stage_rows(min(2, tb - 1))

    def first_sa(j, acc):
        aj = row(ch_scr.at[0, 5], j)
        return tuple(acc[ig] + s_scr[ig, j] * aj for ig in range(ng))

    sa0 = lax.fori_loop(0, R_HS, first_sa, zeros)

    def step(t, sa):
        cur = t % 3
        nxt = (t + 1) % 3
        v = ch_scr[cur, 3]
        vs = tuple(v[ig * SUBLANES:(ig + 1) * SUBLANES, :] for ig in range(ng))
        r_t, w_t, k_t, b_t, a_next = (ch_scr.at[cur, 0], ch_scr.at[cur, 1], ch_scr.at[cur, 2], ch_scr.at[cur, 4],
                                      ch_scr.at[nxt, 5])

        def col(j, carry):
            yacc, san = carry
            wj, bj, kj, rj, aj = row(w_t, j), row(b_t, j), row(k_t, j), row(r_t, j), row(a_next, j)
            ynew, snew = [], []
            for ig in range(ng):
                s = s_scr[ig, j] * wj + sa[ig] * bj + vs[ig] * kj
                s_scr[ig, j] = s
                ynew.append(yacc[ig] + s * rj)
                snew.append(san[ig] + s * aj)
            return tuple(ynew), tuple(snew)

        yacc, san = lax.fori_loop(0, R_HS, col, (zeros, zeros), unroll=8)
        y = jnp.concatenate(yacc, axis=0)
        mu = jnp.mean(y, axis=0, keepdims=True)
        d = y - mu
        var = jnp.mean(d * d, axis=0, keepdims=True)
        yn = d * lax.rsqrt(var + R_GN_EPS) * lnw_ref[...] + lnb_ref[...]
        bonus = jnp.sum(ch_scr[cur, 0] * ch_scr[cur, 2] * rk_ref[...], axis=0, keepdims=True) * v
        yo = yn + bonus
        yt = jnp.concatenate([yo, yo], axis=0).T
        for hh in range(nhh):
            even = yt[hh * CH_B:(hh + 1) * CH_B]
            odd = yt[CH_HALF + hh * CH_B:CH_HALF + (hh + 1) * CH_B]
            yflat_scr[hh, t] = jnp.where(low_half, even, odd)
        stage_chain((t + 2) % 3)
        stage_rows(jnp.minimum(t + 3, tb - 1))
        return tuple(san)

    lax.fori_loop(0, tb, step, sa0)

    def ungroup(hh, carry):
        lanes = pl.ds(pl.multiple_of(hh * LANES, LANES), LANES)
        for tg in range(tb // SUBLANES):
            tiles = swap_rows([yflat_scr[hh, tg * SUBLANES + tt] for tt in range(SUBLANES)])
            for b in range(CH_B):
                y_ref[b, tg * SUBLANES:(tg + 1) * SUBLANES, lanes] = tiles[b]
        return carry

    lax.fori_loop(0, nhh, ungroup, 0)

    @pl.when(tblk == nb - 1)
    def _():
        sT_ref[0] = s_scr[...]


def _chain_const(t):
    th = t.reshape(R_HEADS // 2, 2, R_HS).transpose(2, 1, 0)
    return jnp.broadcast_to(th[..., None], (R_HS, 2, R_HEADS // 2, CH_B)).reshape(R_HS, LANES)


def _wkv(r, w, k, v, a, kk, s0, lw, bsz, seq):
    g = bsz // CH_B
    ng = R_HS // SUBLANES
    nhh = R_HEADS // 2
    tb = min(seq, 32)
    nb = seq // tb
    tok =[t.reshape(bsz, seq, D) for t in (r, w, k, v, a, kk)]
    st_shape = (ng, R_HS, SUBLANES, LANES)
    if s0 is None:
        s0c = jnp.zeros((1,) + st_shape, F32)
        s_in = pl.BlockSpec((1,) + st_shape, lambda gi, ti: (0, 0, 0, 0, 0))
    else:
        s0c = s0.reshape(g, CH_B, nhh, 2, ng, SUBLANES, R_HS).transpose(0, 4, 6, 5, 3, 2, 1).reshape((g,) + st_shape)
        s_in = pl.BlockSpec((1,) + st_shape, lambda gi, ti: (gi, 0, 0, 0, 0))
    blk = pl.BlockSpec((CH_B, tb, D), lambda gi, ti: (gi, ti, 0))
    s_out_spec = pl.BlockSpec((1,) + st_shape, lambda gi, ti: (gi, 0, 0, 0, 0))
    cblk = pl.BlockSpec((R_HS, LANES), lambda gi, ti: (0, 0))
    y, s_last = pl.pallas_call(
        functools.partial(_wkv_kernel, tb=tb, nb=nb),
        grid=(g, nb),
        in_specs=[blk] * 6 + [s_in, cblk, cblk, cblk],
        out_specs=[blk, s_out_spec],
        out_shape=[jax.ShapeDtypeStruct((bsz, seq, D), F32),
                   jax.ShapeDtypeStruct((g,) + st_shape, F32)],
        scratch_shapes=[pltpu.VMEM(st_shape, F32),
                        pltpu.VMEM((3, 6, R_HS, LANES), F32),
                        pltpu.VMEM((3, LANES, LANES), F32),
                        pltpu.VMEM((6, nhh, tb, CH_B, LANES), F32),
                        pltpu.VMEM((nhh, tb, CH_B, LANES), F32)],
        compiler_params=_params("arbitrary", "arbitrary"),
        name="wkv7",
    )(*tok, s0c, lw['lnx_w_c'], lw['lnx_b_c'], lw['rk_c'])
    s_out = s_last.reshape(g, ng, R_HS, SUBLANES, 2, nhh, CH_B).transpose(0, 6, 5, 4, 1, 3, 2)
    return y.reshape(bsz * seq, D), s_out.reshape(bsz, R_HEADS, R_HS, R_HS)


def _attn_kernel(q_ref, k_ref, v_ref, o_ref, *, head_major_cols):
    q = q_ref[...]
    scale = X_HEADDIM ** -0.5
    outs = []
    if not head_major_cols:
        tl = q.shape[0]
        n_rows = MEM_LEN * X_HEADS
        k_all = k_ref[0, 0].reshape(n_rows, X_HEADDIM).astype(BF16)
        v_all = v_ref[0, 0].reshape(n_rows, X_HEADDIM).astype(BF16)
        q_all = jnp.concatenate([q[:, h * X_HEADDIM:(h + 1) * X_HEADDIM] for h in range(X_HEADS)], axis=0)
        s = lax.dot_general(q_all.astype(BF16), k_all, (((1,), (1,)), ((), ())), preferred_element_type=F32) * scale
        q_head = lax.broadcasted_iota(jnp.int32, (X_HEADS * tl, n_rows), 0) // tl
        kv_head = lax.broadcasted_iota(jnp.int32, (X_HEADS * tl, n_rows), 1) % X_HEADS
        s = jnp.where(q_head == kv_head, s, -jnp.inf)
        s = s - jnp.max(s, axis=-1, keepdims=True)
        e = jnp.exp(s)
        p = e / jnp.sum(e, axis=-1, keepdims=True)
        o_all = jnp.dot(p.astype(BF16), v_all, preferred_element_type=F32)
        o_ref[...] = jnp.concatenate([o_all[h * tl:(h + 1) * tl] for h in range(X_HEADS)], axis=-1)
        return
    for h in range(X_HEADS):
        sl = slice(h * X_HEADDIM, (h + 1) * X_HEADDIM)
        qh = q[:, sl].astype(BF16)
        kh = k_ref[:, sl].astype(BF16)
        vh = v_ref[:, sl].astype(BF16)
        s = lax.dot_general(qh, kh, (((1,), (1,)), ((), ())), preferred_element_type=F32) * scale
        s = s - jnp.max(s, axis=-1, keepdims=True)
        e = jnp.exp(s)
        p = e / jnp.sum(e, axis=-1, keepdims=True)
        outs.append(jnp.dot(p.astype(BF16), vh, preferred_element_type=F32))
    o_ref[...] = jnp.concatenate(outs, axis=-1)


def _attend(p_q, mem, layer, bsz, seq):
    tl = min(seq, 512)
    nt = seq // tl
    if isinstance(mem, tuple):
        cache_k, cache_v = mem
        cblk = pl.BlockSpec((1, 1, MEM_LEN, X_HEADS, X_HEADDIM), lambda b, t: (layer, b, 0, 0, 0))
        kv_specs, kv_args, head_major_cols = [cblk, cblk], (cache_k, cache_v), False
    else:
        kv_specs = [pl.BlockSpec((MEM_LEN, D), lambda b, t: (b, 0)), pl.BlockSpec((MEM_LEN, D), lambda b, t: (b, 1))]
        kv_args, head_major_cols = (mem, mem), True
    return pl.pallas_call(
        functools.partial(_attn_kernel, head_major_cols=head_major_cols),
        grid=(bsz, nt),
        in_specs=[pl.BlockSpec((tl, D), lambda b, t: (b * nt + t, 0))] + kv_specs,
        out_specs=pl.BlockSpec((tl, D), lambda b, t: (b * nt + t, 0)),
        out_shape=jax.ShapeDtypeStruct((bsz * seq, D), F32),
        compiler_params=_params("arbitrary", "arbitrary"),
        name="mem_attn",
    )(p_q, *kv_args)


def _merge_kernel(x_ref, ya_ref, yb_ref, g_ref, yc_ref, gate_ref, wmo_ref, wro_ref, wxo_ref, wo_ref,
                  lng_ref, lnb_ref, wr_ref, br_ref, x1_ref, x1b_ref, lg_ref):
    gate = gate_ref[...]
    ma = jnp.dot(ya_ref[...].astype(BF16), wmo_ref[...], preferred_element_type=F32)
    mb = jnp.dot((yb_ref[...] * g_ref[...]).astype(BF16), wro_ref[...], preferred_element_type=F32)
    mc = jnp.dot(yc_ref[...].astype(BF16), wxo_ref[...], preferred_element_type=F32)
    merged = (_sigmoid(gate[:, 0:D]) * ma + _sigmoid(gate[:, D:2 * D]) * mb) + _sigmoid(gate[:, 2 * D:3 * D]) * mc
    h = ALPHA * x_ref[...] + jnp.dot(merged.astype(BF16), wo_ref[...], preferred_element_type=F32)
    x1 = _layer_norm(h, lng_ref[...], lnb_ref[...])
    x1_ref[...] = x1
    x1b_ref[...] = x1.astype(BF16)
    lg_ref[...] =jnp.dot(x1, wr_ref[...], precision=HIGHEST, preferred_element_type=F32) + br_ref[...]


def _merge(x, y_a, y_b, g_b, y_c, p_gate, lw):
    m = x.shape[0]
    tm = min(m, 256)
    row = lambda w: pl.BlockSpec((tm, w), lambda i: (i, 0))
    full = lambda a, b: pl.BlockSpec((a, b), lambda i: (0, 0))
    return pl.pallas_call(
        _merge_kernel,
        grid=(m // tm,),
        in_specs=[row(D), row(M_INNER), row(D), row(D), row(D), row(3 * D),
                  full(M_INNER, D), full(D, D), full(D, D), full(D, D),
                  full(1, D), full(1, D), full(D, LANES), full(1, LANES)],
        out_specs=[row(D), row(D), row(LANES)],
        out_shape=[jax.ShapeDtypeStruct((m, D), F32), jax.ShapeDtypeStruct((m, D), BF16),
                   jax.ShapeDtypeStruct((m, LANES), F32)],
        compiler_params=_params("arbitrary"),
        name="merge_ln_router",
    )(x, y_a, y_b, g_b, y_c, p_gate, lw['w_mo_bf'], lw['w_ro_bf'], lw['w_xo_bf'], lw['w_o_bf'],
      lw['ln1_g'][None, :], lw['ln1_b'][None, :], lw['w_router'], lw['b_router'])


def _moe_kernel(be_ref, nu_ref, x_ref, wg_ref, wu_ref, wd_ref, o_ref, wg_bf, wu_bf, wd_bf):
    i = pl.program_id(0)
    active = i < nu_ref[0]
    new_expert = jnp.logical_or(i == 0, be_ref[i] != be_ref[jnp.maximum(i - 1, 0)])

    @pl.when(jnp.logical_and(active, new_expert))
    def _():
        wg_bf[...] = wg_ref[0, 0].astype(BF16)
        wu_bf[...] = wu_ref[0, 0].astype(BF16)
        wd_bf[...] = wd_ref[0, 0].astype(BF16)

    @pl.when(active)
    def _():
        xb = x_ref[...]
        hid = _silu(jnp.dot(xb, wg_bf[...], preferred_element_type=F32)) * jnp.dot(xb, wu_bf[...],
                                                                                    preferred_element_type=F32)
        o_ref[...] = jnp.dot(hid.astype(BF16), wd_bf[...], preferred_element_type=F32)

    @pl.when(jnp.logical_not(active))
    def _():
        o_ref[...] = jnp.zeros(o_ref.shape, F32)


def _moe_experts(xb, blk_exp, n_used, w_gate, w_up, w_down, layer, blk):
    n_blk = xb.shape[0] // blk
    grid_spec = pltpu.PrefetchScalarGridSpec(
        num_scalar_prefetch=2,
        grid=(n_blk,),
        in_specs=[pl.BlockSpec((blk, D), lambda i, be, nu: (i, 0)),
                  pl.BlockSpec((1, 1, D, D_EXPERT), lambda i, be, nu: (layer, be[i], 0, 0)),
                  pl.BlockSpec((1, 1, D, D_EXPERT), lambda i, be, nu: (layer, be[i], 0, 0)),
                  pl.BlockSpec((1, 1, D_EXPERT, D), lambda i, be, nu: (layer, be[i], 0, 0))],
        out_specs=pl.BlockSpec((blk, D), lambda i, be, nu: (i, 0)),
        scratch_shapes=[pltpu.VMEM((D, D_EXPERT), BF16), pltpu.VMEM((D, D_EXPERT), BF16),
                        pltpu.VMEM((D_EXPERT, D), BF16)],
    )
    return pl.pallas_call(
        _moe_kernel,
        grid_spec=grid_spec,
        out_shape=jax.ShapeDtypeStruct((n_blk * blk, D), F32),
        compiler_params=_params("arbitrary"),
        name="moe_experts",
    )(blk_exp, n_used, xb, w_gate, w_up, w_down)


def _combine_kernel(x_ref, y0_ref, y1_ref, gt_ref, lng_ref, lnb_ref, o_ref, ob_ref):
    gt = gt_ref[...]
    moe = gt[:, 0:1] * y0_ref[...] + gt[:, 1:2] * y1_ref[...]
    x2 = _layer_norm(ALPHA * x_ref[...] + moe, lng_ref[...], lnb_ref[...])
    o_ref[...] = x2
    ob_ref[...] = x2.astype(BF16)


def _combine(x1, y0, y1, gates, lw):
    m = x1.shape[0]
    tm = min(m, 512)
    row = lambda w: pl.BlockSpec((tm, w), lambda i: (i, 0))
    full = lambda a, b: pl.BlockSpec((a, b), lambda i: (0, 0))
    return pl.pallas_call(
        _combine_kernel,
        grid=(m // tm,),
        in_specs=[row(D), row(D), row(D), row(LANES), full(1, D), full(1, D)],
        out_specs=[row(D), row(D)],
        out_shape=[jax.ShapeDtypeStruct((m, D), F32), jax.ShapeDtypeStruct((m, D), BF16)],
        compiler_params=_params("arbitrary"),
        name="moe_combine_ln",
    )(x1, y0, y1, gates, lw['ln2_g'][None, :], lw['ln2_b'][None, :])


def _hier_moe_ln(x1, x1_bf, logits, lw, w_experts, layer):
    t = x1.shape[0]
    blk = 256 if t >= 8192 else 128
    lg = logits[:, :N_EGROUPS]
    le = logits[:, N_EGROUPS:N_EGROUPS + N_EXPERTS].reshape(t, N_EGROUPS, E_PER_GROUP)
    g_sel = jnp.argmax(lg, axis=-1).astype(jnp.int32)
    g_prob = jnp.take_along_axis(jax.nn.softmax(lg, axis=-1), g_sel[:, None], axis=-1)
    le = jnp.take_along_axis(le, g_sel[:, None, None], axis=1)[:, 0]
    top_v, top_i = lax.top_k(le, 2)
    gate = g_prob * jax.nn.softmax(top_v, axis=-1)
    flat_e = (g_sel[:, None] * E_PER_GROUP + top_i.astype(jnp.int32)).reshape(-1)
    n_assign = 2 * t
    order = jnp.argsort(flat_e).astype(jnp.int32)
    onehot = (flat_e[:, None] == jnp.arange(N_EXPERTS, dtype=jnp.int32)[None, :]).astype(jnp.int32)
    seen = jnp.cumsum(onehot, axis=0)
    counts = seen[-1]
    rank = jnp.sum(seen * onehot, axis=1) - 1
    padded = (counts + blk - 1) // blk * blk
    pend = jnp.cumsum(padded)
    pstart = pend - padded
    cstart = jnp.cumsum(counts) - counts
    dest = pstart[flat_e] + rank
    n_blk = n_assign // blk + N_EXPERTS
    blk_start = jnp.arange(n_blk, dtype=jnp.int32) * blk
    blk_exp = jnp.minimum(jnp.sum(pend[None, :] <= blk_start[:, None], axis=1), N_EXPERTS - 1).astype(jnp.int32)
    slot = jnp.arange(n_blk * blk, dtype=jnp.int32)
    slot_e = jnp.repeat(blk_exp, blk)
    off = slot - pstart[slot_e]
    src = order[jnp.clip(cstart[slot_e] + off, 0, n_assign - 1)] // 2
    slot_ok = (off < counts[slot_e]) & (slot < pend[-1])
    slot_tok = jnp.where(slot_ok, src, 0)
    xb = x1_bf[slot_tok]
    n_used = (pend[-1] // blk).astype(jnp.int32)[None]
    yb = _moe_experts(xb, blk_exp, n_used, *w_experts, layer, blk)
    dest2 = dest.reshape(t, 2)
    gates = jnp.pad(gate, ((0, 0), (0, LANES - 2)))
    return _combine(x1, yb[dest2[:, 0]], yb[dest2[:, 1]], gates, lw)


def _trunk_layer(x, x_bf, lw, layer, mem, conv_buf, ssm_all, shift_buf, wkv_s, w_experts, bsz, seq):
    tm = 512
    p_z = _proj(x_bf, lw['w_in_z'], tm, 2048)
    p_xbc = _proj(x_bf, lw['w_in_xbc'], tm, 2048)
    p_dt = _proj(x_bf, lw['w_in_dt'], tm, LANES)
    p_rwkv = _proj(x_bf, lw['w_in_rwkv'], tm, R_SHIFT_W // 2)
    p_q = _proj(x_bf, lw['w_in_q'], tm, 1024)
    p_gate = _proj(x_bf, lw['w_in_gate'], tm, 1536)

    y_a, ssm_new = _mamba(p_z, p_xbc, p_dt, conv_buf, ssm_all, layer, lw, bsz, seq)
    conv_new = p_xbc.reshape(bsz, seq, M_CONV_DIM)[:, seq - (M_CONV - 1):]
    r, w, k, v, a, kk, g_b = _rwkv_prep(p_rwkv, shift_buf, lw, bsz, seq)
    y_b, wkv_new = _wkv(r, w, k, v, a, kk, wkv_s, lw, bsz, seq)
    shift_new = p_rwkv.reshape(bsz, seq, R_SHIFT_W)[:, -1]
    y_c = _attend(p_q, mem, layer, bsz, seq)
    x1, x1_bf, logits = _merge(x, y_a, y_b, g_b, y_c, p_gate, lw)
    x2, x2_bf = _hier_moe_ln(x1, x1_bf, logits, lw, w_experts, layer)
    return x2, x2_bf, conv_new, ssm_new, shift_new, wkv_new


def _layer_weights(l, w):
    w_in = w['w_in'][l]
    lw = {}
    lw['w_in_z'] = w_in[:, :OFF_XBC].astype(BF16)
    lw['w_in_xbc'] = w_in[:, OFF_XBC:OFF_DT].astype(BF16)
    lw['w_in_dt'] = jnp.pad(w_in[:, OFF_DT:OFF_RWKV], ((0, 0), (0, LANES - M_HEADS))).astype(BF16)
    lw['w_in_rwkv'] = w_in[:, OFF_RWKV:OFF_Q].astype(BF16)
    lw['w_in_q'] = w_in[:, OFF_Q:OFF_GATE].astype(BF16)
    lw['w_in_gate'] = w_in[:, OFF_GATE:].astype(BF16)
    lw['conv_w'] = w['conv_w'][l]
    lw['conv_b'] = w['conv_b'][l]
    lw['dt_bias_pad'] = jnp.pad(w['dt_bias'][l], (0, LANES - M_HEADS))[None, :]
    lw['a_log_pad'] = jnp.pad(w['a_log'][l], (0, LANES - M_HEADS))[None, :]
    lw['d_skip_t'] = jnp.broadcast_to(jnp.repeat(w['d_skip'][l], M_HEADDIM)[:, None], (M_INNER, LANES))
    lw['m_norm_w'] = w['m_norm_w'][l]
    lw['r_mu'] = w['r_mu'][l]
    lw['r_w0'] = w['r_w0'][l]
    lw['r_a0'] = w['r_a0'][l]
    zeros64 = jnp.zeros((R_LORA_WA // 2, D), F32)
    lw['r_w2_pad'] = jnp.concatenate([w['r_w2'][l], zeros64], axis=0).astype(BF16)
    lw['r_a2_pad'] = jnp.concatenate([zeros64, w['r_a2'][l]], axis=0).astype(BF16)
    lw['r_g2_bf'] = w['r_g2'][l].astype(BF16)
    lw['r_kk'] = w['r_kk'][l]
    lw['r_ka'] = w['r_ka'][l]
    lw['lnx_w_c'] = _chain_const(w['r_lnx_w'][l])
    lw['lnx_b_c'] = _chain_const(w['r_lnx_b'][l])
    lw['rk_c'] = _chain_const(w['r_rk'][l].reshape(-1))
    lw['w_mem_kv_bf'] = w['w_mem_kv'][l].astype(BF16)
    lw['w_mo_bf'] = w['w_mo'][l].astype(BF16)
    lw['w_ro_bf'] = w['w_ro'][l].astype(BF16)
    lw['w_xo_bf'] = w['w_xo'][l].astype(BF16)
    lw['w_o_bf'] = w['w_o'][l].astype(BF16)
    lw['ln1_g'] = w['ln1_g'][l]
    lw['ln1_b'] = w['ln1_b'][l]
    lw['ln2_g'] = w['ln2_g'][l]
    lw['ln2_b'] = w['ln2_b'][l]
    nr = N_EGROUPS + N_EXPERTS
    lw['w_router'] = jnp.pad(jnp.concatenate([w['w_rg'][l], w['w_re'][l]], axis=1), ((0, 0), (0, LANES - nr)))
    lw['b_router'] = jnp.pad(jnp.concatenate([w['b_rg'][l], w['b_re'][l]]), (0, LANES - nr))[None, :]
    return lw


def kernel(x_prompt, x_sample, mem_prompt, state_ssm, state_conv, state_wkv, state_shift, cache_mem_k, cache_mem_v, w_in, conv_w, conv_b, dt_bias, a_log, d_skip, m_norm_w, r_mu, r_w0, r_w2, r_a0, r_a2, r_g2, r_kk, r_ka, r_rk, r_lnx_w, r_lnx_b, w_mem_kv, w_mo, w_ro, w_xo, w_o, ln1_g, ln1_b, w_rg, b_rg, w_re, b_re, w_gate, w_up, w_down, ln2_g, ln2_b):
    w = dict(w_in=w_in, conv_w=conv_w, conv_b=conv_b, dt_bias=dt_bias, a_log=a_log, d_skip=d_skip,
             m_norm_w=m_norm_w, r_mu=r_mu, r_w0=r_w0, r_w2=r_w2, r_a0=r_a0, r_a2=r_a2, r_g2=r_g2, r_kk=r_kk,
             r_ka=r_ka, r_rk=r_rk, r_lnx_w=r_lnx_w, r_lnx_b=r_lnx_b, w_mem_kv=w_mem_kv, w_mo=w_mo, w_ro=w_ro,
             w_xo=w_xo, w_o=w_o, ln1_g=ln1_g, ln1_b=ln1_b, w_rg=w_rg, b_rg=b_rg, w_re=w_re, b_re=b_re,
             w_gate=w_gate, w_up=w_up, w_down=w_down, ln2_g=ln2_g, ln2_b=ln2_b)
    bp, sp, _ = x_prompt.shape
    bs, ss, _ = x_sample.shape
    n_mem = mem_prompt.shape[1]
    xp = x_prompt.reshape(bp * sp, D)
    xs = x_sample.reshape(bs * ss, D)
    mem_bf = mem_prompt.reshape(bp * n_mem, D).astype(BF16)
    xp_bf = xp.astype(BF16)
    xs_bf = xs.astype(BF16)
    w_experts = (w_gate, w_up, w_down)
    outs = {n: [] for n in ('p_ssm', 'p_conv', 'p_wkv', 'p_shift', 'p_mk', 'p_mv', 's_ssm', 's_conv', 's_wkv',
                            's_shift')}
    for l in range(DEPTH):
        lw = _layer_weights(l, w)
        mkv = _proj(mem_bf, lw['w_mem_kv_bf'], 512, 2048)
        xp, xp_bf, cp, hp, shp, wp = _trunk_layer(xp, xp_bf, lw, l, mkv, None, None, None, None, w_experts, bp, sp)
        xs, xs_bf, cs, hs, shs, ws = _trunk_layer(xs, xs_bf, lw, l, (cache_mem_k, cache_mem_v), state_conv[l],
                                                  state_ssm, state_shift[l], state_wkv[l], w_experts, bs, ss)
        outs['p_ssm'].append(hp)
        outs['p_conv'].append(cp)
        outs['p_wkv'].append(wp)
        outs['p_shift'].append(shp)
        outs['p_mk'].append(mkv[:, :D].reshape(bp, n_mem, X_HEADS, X_HEADDIM))
        outs['p_mv'].append(mkv[:, D:].reshape(bp, n_mem, X_HEADS, X_HEADDIM))
        outs['s_ssm'].append(hs)
        outs['s_conv'].append(cs)
        outs['s_wkv'].append(ws)
        outs['s_shift'].append(shs)
    st = lambda n: jnp.stack(outs[n])
    return (xp.reshape(bp, sp, D), xs.reshape(bs, ss, D), st('p_ssm'), st('p_conv'), st('p_wkv'), st('p_shift'),
            st('p_mk'), st('p_mv'), st('s_ssm'), st('s_conv'), st('s_wkv'), st('s_shift'))
```

```python
import functools

import jax
import jax.numpy as jnp
from jax import lax
from jax.experimental import pallas as pl
from jax.experimental.pallas import tpu as pltpu

F32 = jnp.float32
BF16 = jnp.bfloat16
HIGHEST = lax.Precision.HIGHEST

LANES = 128
SUBLANES = 8
VMEM_LIMIT = 56 * 1024 * 1024

D = 1024
DEPTH = 4
M_INNER = 2048
M_HEADDIM = 64
M_HEADS = 32
M_GROUPS = 8
M_HPG = M_HEADS // M_GROUPS
M_STATE = 128
M_CONV = 4
M_CONV_DIM = M_INNER + 2 * M_GROUPS * M_STATE
M_CHUNK = 128
RMS_EPS = 1e-5
R_HEADS = 16
R_HS = 64
R_LORA_WA = 128
R_LORA_G = 128
R_SHIFT_W = 3 * D + R_LORA_WA + R_LORA_G
R_GN_EPS = 64e-5
MEM_LEN = 256
X_HEADS = 4
X_HEADDIM = 256
OFF_XBC = M_INNER
OFF_DT = OFF_XBC + M_CONV_DIM
OFF_RWKV = OFF_DT + M_HEADS
OFF_Q = OFF_RWKV + R_SHIFT_W
OFF_GATE = OFF_Q + D
N_EGROUPS = 4
E_PER_GROUP = 8
N_EXPERTS = 32
D_EXPERT = 512
LN_EPS = 1e-5
ALPHA = (2 * DEPTH) ** 0.25


def _params(*sem):
    return pltpu.CompilerParams(dimension_semantics=sem, vmem_limit_bytes=VMEM_LIMIT)


def _sigmoid(x):
    return 1.0 / (1.0 + jnp.exp(-x))


def _silu(x):
    return x * _sigmoid(x)


def _softplus(x):
    return jnp.maximum(x, 0.0) + jnp.log1p(jnp.exp(-jnp.abs(x)))


def _layer_norm(h, g, b):
    mu = jnp.mean(h, axis=-1, keepdims=True)
    d = h - mu
    var = jnp.mean(d * d, axis=-1, keepdims=True)
    return d * lax.rsqrt(var + LN_EPS) * g + b


def _proj_kernel(x_ref, w_ref, o_ref):
    o_ref[...] = jnp.dot(x_ref[...], w_ref[...], preferred_element_type=F32)


def _proj(x, w, tm, tn):
    m, k = x.shape
    n = w.shape[1]
    tm = min(tm, m)
    tn = min(tn, n)
    return pl.pallas_call(
        _proj_kernel,
        grid=(n // tn, m // tm),
        in_specs=[pl.BlockSpec((tm, k), lambda j, i: (i, 0)),
                  pl.BlockSpec((k, tn), lambda j, i: (0, j))],
        out_specs=pl.BlockSpec((tm, tn), lambda j, i: (i, j)),
        out_shape=jax.ShapeDtypeStruct((m, n), F32),
        compiler_params=_params("arbitrary", "arbitrary"),
        name="proj",
    )(x, w)


def _mamba_kernel(z_ref, xbc_ref, dt_ref, cbuf_ref, h0_ref, convw_ref, convb_ref, dtb_ref, alog_ref,
                  dskip_ref, mnw_ref, y_ref, hT_ref, ubuf, h_scr, yT_scr, *, rows, nc, q):
    c = pl.program_id(1)

    @pl.when(c == 0)
    def _():
        if rows < q:
            ubuf[...] = jnp.zeros(ubuf.shape, F32)
        ubuf[0:SUBLANES, :] = cbuf_ref[0]
        h_scr[...] = h0_ref[0, 0]

    ubuf[SUBLANES:SUBLANES + rows, :] = xbc_ref[...]
    cw = convw_ref[...]
    acc = ubuf[5:5 + q, :] * cw[0:1]
    acc = acc + ubuf[6:6 + q, :] * cw[1:2]
    acc = acc + ubuf[7:7 + q, :] * cw[2:3]
    acc = acc + ubuf[8:8 + q, :] * cw[3:4]
    acc = acc + convb_ref[...]
    if nc > 1:
        ubuf[0:SUBLANES, :] = ubuf[q:q + SUBLANES, :]
    act = _silu(acc)

    row_id = lax.broadcasted_iota(jnp.int32, (q, q), 0)
    col_id = lax.broadcasted_iota(jnp.int32, (q, q), 1)
    dt_raw = dt_ref[...]
    if rows < q:
        dt_raw = jnp.concatenate([dt_raw, jnp.zeros((q - rows, LANES), F32)], axis=0)
    dt = _softplus(dt_raw + dtb_ref[...])
    a_neg = -jnp.exp(alog_ref[...])
    adt = a_neg * dt
    if rows < q:
        valid = lax.broadcasted_iota(jnp.int32, (q, 1), 0) < rows
        act = jnp.where(valid, act, 0.0)
        adt = jnp.where(valid, adt, 0.0)
    tril = (row_id >= col_id).astype(F32)
    acs = jnp.dot(tril, adt, precision=HIGHEST, preferred_element_type=F32)
    acs_t = acs.T
    dt_t = dt.T
    xs_t = act[:, :M_INNER].T
    upper = row_id <= col_id

    for g in range(M_GROUPS):
        b_g = act[:, M_INNER + g * M_STATE:M_INNER + (g + 1) * M_STATE]
        c_g = act[:, M_INNER + (M_GROUPS + g) * M_STATE:M_INNER + (M_GROUPS + g + 1) * M_STATE]
        b_bf = b_g.astype(BF16)
        c_t = c_g.T
        c_t_bf = c_t.astype(BF16)
        cb_t = jnp.dot(b_bf, c_t_bf, preferred_element_type=F32)
        for r in range(M_HPG):
            h = g * M_HPG + r
            a_row = acs_t[h:h + 1, :]
            a_col = acs[:, h:h + 1]
            lm_t = jnp.exp(jnp.where(upper, a_row - a_col, -jnp.inf))
            m_t = (cb_t * lm_t).astype(BF16)
            cs_t = (c_t * jnp.exp(a_row)).astype(BF16)
            x_t = xs_t[h * M_HEADDIM:(h + 1) * M_HEADDIM, :]
            xdt_t = x_t * dt_t[h:h + 1, :]
            hprev = h_scr[h]
            y_t = (jnp.dot(xdt_t.astype(BF16), m_t, preferred_element_type=F32)
                   + jnp.dot(hprev.astype(BF16), cs_t, preferred_element_type=F32))
            yT_scr[h * M_HEADDIM:(h + 1) * M_HEADDIM, :] = y_t + x_t * dskip_ref[h * M_HEADDIM:(h + 1) * M_HEADDIM, 0:q]
            a_last = acs[q - 1:q, h:h + 1]
            wrow = jnp.exp(a_last - a_row)
            s_c = jnp.dot((xdt_t * wrow).astype(BF16), b_bf, preferred_element_type=F32)
            h_scr[h] = hprev * jnp.exp(a_last) + s_c

    y = yT_scr[...].T
    if rows < q:
        y = y[0:rows, :]
    y = y * _silu(z_ref[...])
    gw = M_INNER // M_GROUPS
    parts = []
    for g in range(M_GROUPS):
        yg = y[:, g * gw:(g + 1) * gw]
        ms = jnp.mean(yg * yg, axis=-1, keepdims=True)
        parts.append(yg * lax.rsqrt(ms + RMS_EPS))
    y_ref[...] = jnp.concatenate(parts, axis=-1) * mnw_ref[...]

    @pl.when(c == nc - 1)
    def _():
        hT_ref[0] = h_scr[...]


def _mamba(p_z, p_xbc, p_dt, conv_buf, h0_all, layer, lw, bsz, seq):
    if seq >= M_CHUNK:
        q = rows = M_CHUNK
        nc = seq // q
    else:
        q = rows = seq
        nc = 1
    if conv_buf is None:
        cbuf = jnp.zeros((1, SUBLANES, M_CONV_DIM), F32)
        cbuf_map = lambda b, c: (0, 0, 0)
    else:
        cbuf = jnp.pad(conv_buf, ((0, 0), (SUBLANES - (M_CONV - 1), 0), (0, 0)))
        cbuf_map = lambda b, c: (b, 0, 0)
    if h0_all is None:
        h0_all = jnp.zeros((1, 1, M_HEADS, M_HEADDIM, M_STATE), F32)
        h0_map = lambda b, c: (0, 0, 0, 0, 0)
    else:
        h0_map = lambda b, c: (layer, b, 0, 0, 0)
    convw = jnp.pad(lw['conv_w'], ((0, SUBLANES - M_CONV), (0, 0)))
    kern = functools.partial(_mamba_kernel, rows=rows, nc=nc, q=q)
    row_map = lambda b, c: (b * nc + c, 0)
    const2 = lambda b, c: (0, 0)
    y, h_last = pl.pallas_call(
        kern,
        grid=(bsz, nc),
        in_specs=[pl.BlockSpec((rows, M_INNER), row_map),
                  pl.BlockSpec((rows, M_CONV_DIM), row_map),
                  pl.BlockSpec((rows, LANES), row_map),
                  pl.BlockSpec((1, SUBLANES, M_CONV_DIM), cbuf_map),
                  pl.BlockSpec((1, 1, M_HEADS, M_HEADDIM, M_STATE), h0_map),
                  pl.BlockSpec((SUBLANES, M_CONV_DIM), const2),
                  pl.BlockSpec((1, M_CONV_DIM), const2),
                  pl.BlockSpec((1, LANES), const2),
                  pl.BlockSpec((1, LANES), const2),
                  pl.BlockSpec((M_INNER, LANES), const2),
                  pl.BlockSpec((1, M_INNER), const2)],
        out_specs=[pl.BlockSpec((rows, M_INNER), row_map),
                   pl.BlockSpec((1, M_HEADS, M_HEADDIM, M_STATE), lambda b, c: (b, 0, 0, 0))],
        out_shape=[jax.ShapeDtypeStruct((bsz * seq, M_INNER), F32),
                   jax.ShapeDtypeStruct((bsz, M_HEADS, M_HEADDIM, M_STATE), F32)],
        scratch_shapes=[pltpu.VMEM((q + 2 * SUBLANES, M_CONV_DIM), F32),
                        pltpu.VMEM((M_HEADS, M_HEADDIM, M_STATE), F32),
                        pltpu.VMEM((M_INNER, q), F32)],
        compiler_params=_params("arbitrary", "arbitrary"),
        name="mamba_ssd",
    )(p_z, p_xbc, p_dt, cbuf, h0_all, convw, lw['conv_b'][None, :], lw['dt_bias_pad'], lw['a_log_pad'],
      lw['d_skip_t'], lw['m_norm_w'][None, :])
    return y, h_last


def _rwkv_prep_kernel(cols_ref, shift_ref, mu_ref, w0_ref, w2_ref, a0_ref, a2_ref, g2_ref, kkw_ref, kaw_ref,
                      r_out, w_out, k_out, v_out, a_out, kk_out, g_out, sbuf, *, tl, nt):
    t = pl.program_id(1)

    @pl.when(t == 0)
    def _():
        sbuf[0:SUBLANES, :] = shift_ref[0]

    cols = cols_ref[...]
    sbuf[SUBLANES:SUBLANES + tl, :] = cols
    prev = sbuf[SUBLANES - 1:SUBLANES - 1 + tl, :]
    if nt > 1:
        sbuf[0:SUBLANES, :] = sbuf[tl:tl + SUBLANES, :]
    mixed = cols + (prev - cols) * mu_ref[...]
    r = mixed[:, 0:D]
    k = mixed[:, D:2 * D]
    v = mixed[:, 2 * D:3 * D]
    wa = mixed[:, 3 * D:3 * D + R_LORA_WA]
    gl = mixed[:, 3 * D + R_LORA_WA:]
    lw_ = jnp.dot(jnp.tanh(wa).astype(BF16), w2_ref[...], preferred_element_type=F32)
    la_ = jnp.dot(wa.astype(BF16), a2_ref[...], preferred_element_type=F32)
    g = jnp.dot(_sigmoid(gl).astype(BF16), g2_ref[...], preferred_element_type=F32)
    w_log = -_softplus(-(w0_ref[...] + lw_)) - 0.5
    decay = jnp.exp(-jnp.exp(w_log))
    a = _sigmoid(a0_ref[...] + la_)
    r_out[...] = r
    w_out[...] = decay
    k_out[...] = k * (1.0 + (a - 1.0) * kaw_ref[...])
    v_out[...] = v
    a_out[...] = a
    kk_out[...] = k * kkw_ref[...]
    g_out[...] = g


def _rwkv_prep(p_rwkv, shift_buf, lw, bsz, seq):
    tl = min(seq, 256)
    nt = seq // tl
    if shift_buf is None:
        sb = jnp.zeros((1, SUBLANES, R_SHIFT_W), F32)
        sb_map = lambda b, t: (0, 0, 0)
    else:
        sb = jnp.pad(shift_buf[:, None, :], ((0, 0), (SUBLANES - 1, 0), (0, 0)))
        sb_map = lambda b, t: (b, 0, 0)
    row_map = lambda b, t: (b * nt + t, 0)
    const2 = lambda b, t: (0, 0)
    vec = pl.BlockSpec((1, D), const2)
    outs = pl.pallas_call(
        functools.partial(_rwkv_prep_kernel, tl=tl, nt=nt),
        grid=(bsz, nt),
        in_specs=[pl.BlockSpec((tl, R_SHIFT_W), row_map),
                  pl.BlockSpec((1, SUBLANES, R_SHIFT_W), sb_map),
                  pl.BlockSpec((1, R_SHIFT_W), const2),
                  vec, pl.BlockSpec((R_LORA_WA, D), const2),
                  vec, pl.BlockSpec((R_LORA_WA, D), const2),
                  pl.BlockSpec((R_LORA_G, D), const2),
                  vec, vec],
        out_specs=[pl.BlockSpec((tl, D), row_map)] * 7,
        out_shape=[jax.ShapeDtypeStruct((bsz * seq, D), F32)] * 7,
        scratch_shapes=[pltpu.VMEM((tl + 2 * SUBLANES, R_SHIFT_W), F32)],
        compiler_params=_params("arbitrary", "arbitrary"),
        name="rwkv_prep",
    )(p_rwkv, sb, lw['r_mu'][None, :], lw['r_w0'][None, :], lw['r_w2_pad'], lw['r_a0'][None, :], lw['r_a2_pad'],
      lw['r_g2_bf'], lw['r_kk'][None, :], lw['r_ka'][None, :])
    return outs


CH_B = LANES // R_HEADS
CH_HALF = LANES // 2


def _wkv_kernel(r_ref, w_ref, k_ref, v_ref, a_ref, kk_ref, s0_ref, lnw_ref, lnb_ref, rk_ref,
                y_ref, sT_ref, s_scr, ch_scr, m_scr, tok_scr, yflat_scr, *, tb, nb):
    ng = R_HS // SUBLANES
    nhh = R_HEADS // 2
    tblk = pl.program_id(1)
    low_half = lax.broadcasted_iota(jnp.int32, (SUBLANES, LANES), 1) < CH_HALF

    @pl.when(tblk == 0)
    def _():
        s_scr[...] = s0_ref[0]

    sub_id = lax.broadcasted_iota(jnp.int32, (SUBLANES, LANES), 0)

    def swap_rows(vs):
        for s in (1, 2, 4):
            keep_low = (sub_id & s) == 0
            nxt_vs = []
            for i in range(SUBLANES):
                other = vs[i ^ s]
                if i & s:
                    nxt_vs.append(jnp.where(keep_low, pltpu.roll(other, SUBLANES - s, axis=0), vs[i]))
                else:
                    nxt_vs.append(jnp.where(keep_low, vs[i], pltpu.roll(other, s, axis=0)))
            vs = nxt_vs
        return vs

    for idx, ref in enumerate((r_ref, w_ref, k_ref, v_ref, a_ref, kk_ref)):
        def regroup(hh, carry, idx=idx, ref=ref):
            lanes = pl.ds(pl.multiple_of(hh * LANES, LANES), LANES)
            for tg in range(tb // SUBLANES):
                tiles = swap_rows([ref[b, tg * SUBLANES:(tg + 1) * SUBLANES, lanes] for b in range(CH_B)])
                for tt in range(SUBLANES):
                    tok_scr[idx, hh, tg * SUBLANES + tt] = tiles[tt]
            return carry

        lax.fori_loop(0, nhh, regroup, 0)

    def rows_at(idx, t, hh):
        return tok_scr[idx, hh, t]

    def stage_rows(t):
        for p in range(3):
            for hh in range(nhh):
                av, bv = rows_at(2 * p, t, hh), rows_at(2 * p + 1, t, hh)
                m_scr[p, hh * CH_B:(hh + 1) * CH_B, :] = jnp.where(low_half, av, pltpu.roll(bv, CH_HALF, axis=1))
                m_scr[p, CH_HALF + hh * CH_B:CH_HALF + (hh + 1) * CH_B, :] = jnp.where(
                    low_half, pltpu.roll(av, CH_HALF, axis=1), bv)

    def stage_chain(slot):
        rw, kv, akk = m_scr[0].T, m_scr[1].T, m_scr[2].T
        kkr = akk[R_HS:]
        nrm = jnp.sqrt(jnp.sum(kkr * kkr, axis=0, keepdims=True))
        kk = kkr / jnp.maximum(nrm, 1e-12)
        for idx, val in enumerate((rw[:R_HS], rw[R_HS:], kv[:R_HS], kv[R_HS:], kk * akk[:R_HS], -kk)):
            ch_scr[slot, idx] = val

    def row(ref_view, j):
        return jnp.broadcast_to(ref_view[pl.ds(j, 1), :], (SUBLANES, LANES))

    zeros = tuple(jnp.zeros((SUBLANES, LANES), F32) for _ in range(ng))
    stage_rows(0)
    stage_chain(0)
    stage_rows(min(1, tb - 1))
    stage_chain(1)
    stage_rows(min(2, tb - 1))

    def first_sa(j, acc):
        aj = row(ch_scr.at[0, 5], j)
        return tuple(acc[ig] + s_scr[ig, j] * aj for ig in range(ng))

    sa0 = lax.fori_loop(0, R_HS, first_sa, zeros)

    def step(t, sa):
        cur = t % 3
        nxt = (t + 1) % 3
        v = ch_scr[cur, 3]
        vs = tuple(v[ig * SUBLANES:(ig + 1) * SUBLANES, :] for ig in range(ng))
        r_t, w_t, k_t, b_t, a_next = (ch_scr.at[cur, 0], ch_scr.at[cur, 1], ch_scr.at[cur, 2], ch_scr.at[cur, 4],
                                      ch_scr.at[nxt, 5])

        def col(j, carry):
            yacc, san = carry
            wj, bj, kj, rj, aj = row(w_t, j), row(b_t, j), row(k_t, j), row(r_t, j), row(a_next, j)
            ynew, snew = [], []
            for ig in range(ng):
                s = s_scr[ig, j] * wj + sa[ig] * bj + vs[ig] * kj
                s_scr[ig, j] = s
                ynew.append(yacc[ig] + s * rj)
                snew.append(san[ig] + s * aj)
            return tuple(ynew), tuple(snew)

        yacc, san = lax.fori_loop(0, R_HS, col, (zeros, zeros), unroll=8)
        y = jnp.concatenate(yacc, axis=0)
        mu = jnp.mean(y, axis=0, keepdims=True)
        d = y - mu
        var = jnp.mean(d * d, axis=0, keepdims=True)
        yn = d * lax.rsqrt(var + R_GN_EPS) * lnw_ref[...] + lnb_ref[...]
        bonus = jnp.sum(ch_scr[cur, 0] * ch_scr[cur, 2] * rk_ref[...], axis=0, keepdims=True) * v
        yo = yn + bonus
        yt = jnp.concatenate([yo, yo], axis=0).T
        for hh in range(nhh):
            even = yt[hh * CH_B:(hh + 1) * CH_B]
            odd = yt[CH_HALF + hh * CH_B:CH_HALF + (hh + 1) * CH_B]
            yflat_scr[hh, t] = jnp.where(low_half, even, odd)
        stage_chain((t + 2) % 3)
        stage_rows(jnp.minimum(t + 3, tb - 1))
        return tuple(san)

    lax.fori_loop(0, tb, step, sa0)

    def ungroup(hh, carry):
        lanes = pl.ds(pl.multiple_of(hh * LANES, LANES), LANES)
        for tg in range(tb // SUBLANES):
            tiles = swap_rows([yflat_scr[hh, tg * SUBLANES + tt] for tt in range(SUBLANES)])
            for b in range(CH_B):
                y_ref[b, tg * SUBLANES:(tg + 1) * SUBLANES, lanes] = tiles[b]
        return carry

    lax.fori_loop(0, nhh, ungroup, 0)

    @pl.when(tblk == nb - 1)
    def _():
        sT_ref[0] = s_scr[...]


def _chain_const(t):
    th = t.reshape(R_HEADS // 2, 2, R_HS).transpose(2, 1, 0)
    return jnp.broadcast_to(th[..., None], (R_HS, 2, R_HEADS // 2, CH_B)).reshape(R_HS, LANES)


def _wkv(r, w, k, v, a, kk, s0, lw, bsz, seq):
    g = bsz // CH_B
    ng = R_HS // SUBLANES
    nhh = R_HEADS // 2
    tb = min(seq, 32)
    nb = seq // tb
    tok =[t.reshape(bsz, seq, D) for t in (r, w, k, v, a, kk)]
    st_shape = (ng, R_HS, SUBLANES, LANES)
    if s0 is None:
        s0c = jnp.zeros((1,) + st_shape, F32)
        s_in = pl.BlockSpec((1,) + st_shape, lambda gi, ti: (0, 0, 0, 0, 0))
    else:
        s0c = s0.reshape(g, CH_B, nhh, 2, ng, SUBLANES, R_HS).transpose(0, 4, 6, 5, 3, 2, 1).reshape((g,) + st_shape)
        s_in = pl.BlockSpec((1,) + st_shape, lambda gi, ti: (gi, 0, 0, 0, 0))
    blk = pl.BlockSpec((CH_B, tb, D), lambda gi, ti: (gi, ti, 0))
    s_out_spec = pl.BlockSpec((1,) + st_shape, lambda gi, ti: (gi, 0, 0, 0, 0))
    cblk = pl.BlockSpec((R_HS, LANES), lambda gi, ti: (0, 0))
    y, s_last = pl.pallas_call(
        functools.partial(_wkv_kernel, tb=tb, nb=nb),
        grid=(g, nb),
        in_specs=[blk] * 6 + [s_in, cblk, cblk, cblk],
        out_specs=[blk, s_out_spec],
        out_shape=[jax.ShapeDtypeStruct((bsz, seq, D), F32),
                   jax.ShapeDtypeStruct((g,) + st_shape, F32)],
        scratch_shapes=[pltpu.VMEM(st_shape, F32),
                        pltpu.VMEM((3, 6, R_HS, LANES), F32),
                        pltpu.VMEM((3, LANES, LANES), F32),
                        pltpu.VMEM((6, nhh, tb, CH_B, LANES), F32),
                        pltpu.VMEM((nhh, tb, CH_B, LANES), F32)],
        compiler_params=_params("arbitrary", "arbitrary"),
        name="wkv7",
    )(*tok, s0c, lw['lnx_w_c'], lw['lnx_b_c'], lw['rk_c'])
    s_out = s_last.reshape(g, ng, R_HS, SUBLANES, 2, nhh, CH_B).transpose(0, 6, 5, 4, 1, 3, 2)
    return y.reshape(bsz * seq, D), s_out.reshape(bsz, R_HEADS, R_HS, R_HS)


def _attn_kernel(q_ref, k_ref, v_ref, o_ref, *, head_major_cols):
    q = q_ref[...]
    scale = X_HEADDIM ** -0.5
    outs = []
    if not head_major_cols:
        tl = q.shape[0]
        n_rows = MEM_LEN * X_HEADS
        k_all = k_ref[0, 0].reshape(n_rows, X_HEADDIM).astype(BF16)
        v_all = v_ref[0, 0].reshape(n_rows, X_HEADDIM).astype(BF16)
        q_all = jnp.concatenate([q[:, h * X_HEADDIM:(h + 1) * X_HEADDIM] for h in range(X_HEADS)], axis=0)
        s = lax.dot_general(q_all.astype(BF16), k_all, (((1,), (1,)), ((), ())), preferred_element_type=F32) * scale
        q_head = lax.broadcasted_iota(jnp.int32, (X_HEADS * tl, n_rows), 0) // tl
        kv_head = lax.broadcasted_iota(jnp.int32, (X_HEADS * tl, n_rows), 1) % X_HEADS
        s = jnp.where(q_head == kv_head, s, -jnp.inf)
        s = s - jnp.max(s, axis=-1, keepdims=True)
        e = jnp.exp(s)
        p = e / jnp.sum(e, axis=-1, keepdims=True)
        o_all = jnp.dot(p.astype(BF16), v_all, preferred_element_type=F32)
        o_ref[...] = jnp.concatenate([o_all[h * tl:(h + 1) * tl] for h in range(X_HEADS)], axis=-1)
        return
    for h in range(X_HEADS):
        sl = slice(h * X_HEADDIM, (h + 1) * X_HEADDIM)
        qh = q[:, sl].astype(BF16)
        kh = k_ref[:, sl].astype(BF16)
        vh = v_ref[:, sl].astype(BF16)
        s = lax.dot_general(qh, kh, (((1,), (1,)), ((), ())), preferred_element_type=F32) * scale
        s = s - jnp.max(s, axis=-1, keepdims=True)
        e = jnp.exp(s)
        p = e / jnp.sum(e, axis=-1, keepdims=True)
        outs.append(jnp.dot(p.astype(BF16), vh, preferred_element_type=F32))
    o_ref[...] = jnp.concatenate(outs, axis=-1)


def _attend(p_q, mem, layer, bsz, seq):
    tl = min(seq, 512)
    nt = seq // tl
    if isinstance(mem, tuple):
        cache_k, cache_v = mem
        cblk = pl.BlockSpec((1, 1, MEM_LEN, X_HEADS, X_HEADDIM), lambda b, t: (layer, b, 0, 0, 0))
        kv_specs, kv_args, head_major_cols = [cblk, cblk], (cache_k, cache_v), False
    else:
        kv_specs = [pl.BlockSpec((MEM_LEN, D), lambda b, t: (b, 0)), pl.BlockSpec((MEM_LEN, D), lambda b, t: (b, 1))]
        kv_args, head_major_cols = (mem, mem), True
    return pl.pallas_call(
        functools.partial(_attn_kernel, head_major_cols=head_major_cols),
        grid=(bsz, nt),
        in_specs=[pl.BlockSpec((tl, D), lambda b, t: (b * nt + t, 0))] + kv_specs,
        out_specs=pl.BlockSpec((tl, D), lambda b, t: (b * nt + t, 0)),
        out_shape=jax.ShapeDtypeStruct((bsz * seq, D), F32),
        compiler_params=_params("arbitrary", "arbitrary"),
        name="mem_attn",
    )(p_q, *kv_args)


def _merge_kernel(x_ref, ya_ref, yb_ref, g_ref, yc_ref, gate_ref, wmo_ref, wro_ref, wxo_ref, wo_ref,
                  lng_ref, lnb_ref, wr_ref, br_ref, x1_ref, lg_ref):
    gate = gate_ref[...]
    ma = jnp.dot(ya_ref[...].astype(BF16), wmo_ref[...], preferred_element_type=F32)
    mb = jnp.dot((yb_ref[...] * g_ref[...]).astype(BF16), wro_ref[...], preferred_element_type=F32)
    mc = jnp.dot(yc_ref[...].astype(BF16), wxo_ref[...], preferred_element_type=F32)
    merged = (_sigmoid(gate[:, 0:D]) * ma + _sigmoid(gate[:, D:2 * D]) * mb) + _sigmoid(gate[:, 2 * D:3 * D]) * mc
    h = ALPHA * x_ref[...] + jnp.dot(merged.astype(BF16), wo_ref[...], preferred_element_type=F32)
    x1 = _layer_norm(h, lng_ref[...], lnb_ref[...])
    x1_ref[...] = x1
    lg_ref[...] = jnp.dot(x1.astype(BF16), wr_ref[...], preferred_element_type=F32) + br_ref[...]


def _merge(x, y_a, y_b, g_b, y_c, p_gate, lw):
    m = x.shape[0]
    tm = min(m, 256)
    row = lambda w: pl.BlockSpec((tm, w), lambda i: (i, 0))
    full = lambda a, b: pl.BlockSpec((a, b), lambda i: (0, 0))
    return pl.pallas_call(
        _merge_kernel,
        grid=(m // tm,),
        in_specs=[row(D), row(M_INNER), row(D), row(D), row(D), row(3 * D),
                  full(M_INNER, D), full(D, D), full(D, D), full(D, D),
                  full(1, D), full(1, D), full(D, LANES), full(1, LANES)],
        out_specs=[row(D), row(LANES)],
        out_shape=[jax.ShapeDtypeStruct((m, D), F32), jax.ShapeDtypeStruct((m, LANES), F32)],
        compiler_params=_params("arbitrary"),
        name="merge_ln_router",
    )(x, y_a, y_b, g_b, y_c, p_gate, lw['w_mo_bf'], lw['w_ro_bf'], lw['w_xo_bf'], lw['w_o_bf'],
      lw['ln1_g'][None, :], lw['ln1_b'][None, :], lw['w_router'], lw['b_router'])


def _moe_kernel(be_ref, nu_ref, x_ref, wg_ref, wu_ref, wd_ref, o_ref, wg_bf, wu_bf, wd_bf):
    i = pl.program_id(0)
    active = i < nu_ref[0]
    new_expert = jnp.logical_or(i == 0, be_ref[i] != be_ref[jnp.maximum(i - 1, 0)])

    @pl.when(jnp.logical_and(active, new_expert))
    def _():
        wg_bf[...] = wg_ref[0, 0].astype(BF16)
        wu_bf[...] = wu_ref[0, 0].astype(BF16)
        wd_bf[...] = wd_ref[0, 0].astype(BF16)

    @pl.when(active)
    def _():
        xb = x_ref[...].astype(BF16)
        hid = _silu(jnp.dot(xb, wg_bf[...], preferred_element_type=F32)) * jnp.dot(xb, wu_bf[...],
                                                                                    preferred_element_type=F32)
        o_ref[...] = jnp.dot(hid.astype(BF16), wd_bf[...], preferred_element_type=F32)

    @pl.when(jnp.logical_not(active))
    def _():
        o_ref[...] = jnp.zeros(o_ref.shape, F32)


def _moe_experts(xb, blk_exp, n_used, w_gate, w_up, w_down, layer, blk):
    n_blk = xb.shape[0] // blk
    grid_spec = pltpu.PrefetchScalarGridSpec(
        num_scalar_prefetch=2,
        grid=(n_blk,),
        in_specs=[pl.BlockSpec((blk, D), lambda i, be, nu: (i, 0)),
                  pl.BlockSpec((1, 1, D, D_EXPERT), lambda i, be, nu: (layer, be[i], 0, 0)),
                  pl.BlockSpec((1, 1, D, D_EXPERT), lambda i, be, nu: (layer, be[i], 0, 0)),
                  pl.BlockSpec((1, 1, D_EXPERT, D), lambda i, be, nu: (layer, be[i], 0, 0))],
        out_specs=pl.BlockSpec((blk, D), lambda i, be, nu: (i, 0)),
        scratch_shapes=[pltpu.VMEM((D, D_EXPERT), BF16), pltpu.VMEM((D, D_EXPERT), BF16),
                        pltpu.VMEM((D_EXPERT, D), BF16)],
    )
    return pl.pallas_call(
        _moe_kernel,
        grid_spec=grid_spec,
        out_shape=jax.ShapeDtypeStruct((n_blk * blk, D), F32),
        compiler_params=_params("arbitrary"),
        name="moe_experts",
    )(blk_exp, n_used, xb, w_gate, w_up, w_down)


def _combine_kernel(x_ref, y0_ref, y1_ref, gt_ref, lng_ref, lnb_ref, o_ref, ob_ref):
    gt = gt_ref[...]
    moe = gt[:, 0:1] * y0_ref[...] + gt[:, 1:2] * y1_ref[...]
    x2 = _layer_norm(ALPHA * x_ref[...] + moe, lng_ref[...], lnb_ref[...])
    o_ref[...] = x2
    ob_ref[...] = x2.astype(BF16)


def _combine(x1, y0, y1, gates, lw):
    m = x1.shape[0]
    tm = min(m, 512)
    row = lambda w: pl.BlockSpec((tm, w), lambda i: (i, 0))
    full = lambda a, b: pl.BlockSpec((a, b), lambda i: (0, 0))
    return pl.pallas_call(
        _combine_kernel,
        grid=(m // tm,),
        in_specs=[row(D), row(D), row(D), row(LANES), full(1, D), full(1, D)],
        out_specs=[row(D), row(D)],
        out_shape=[jax.ShapeDtypeStruct((m, D), F32), jax.ShapeDtypeStruct((m, D), BF16)],
        compiler_params=_params("arbitrary"),
        name="moe_combine_ln",
    )(x1, y0, y1, gates, lw['ln2_g'][None, :], lw['ln2_b'][None, :])


def _hier_moe_ln(x1, logits, lw, w_experts, layer):
    t = x1.shape[0]
    blk = 256 if t >= 8192 else 128
    lg = logits[:, :N_EGROUPS]
    le = logits[:, N_EGROUPS:N_EGROUPS + N_EXPERTS].reshape(t, N_EGROUPS, E_PER_GROUP)
    g_sel = jnp.argmax(lg, axis=-1).astype(jnp.int32)
    g_prob = jnp.take_along_axis(jax.nn.softmax(lg, axis=-1), g_sel[:, None], axis=-1)
    le = jnp.take_along_axis(le, g_sel[:, None, None], axis=1)[:, 0]
    top_v, top_i = lax.top_k(le, 2)
    gate = g_prob * jax.nn.softmax(top_v, axis=-1)
    flat_e = (g_sel[:, None] * E_PER_GROUP + top_i.astype(jnp.int32)).reshape(-1)
    n_assign = 2 * t
    order = jnp.argsort(flat_e).astype(jnp.int32)
    onehot = (flat_e[:, None] == jnp.arange(N_EXPERTS, dtype=jnp.int32)[None, :]).astype(jnp.int32)
    seen = jnp.cumsum(onehot, axis=0)
    counts = seen[-1]
    rank = jnp.sum(seen * onehot, axis=1) - 1
    padded = (counts + blk - 1) // blk * blk
    pend = jnp.cumsum(padded)
    pstart = pend - padded
    cstart = jnp.cumsum(counts) - counts
    dest = pstart[flat_e] + rank
    n_blk = n_assign // blk + N_EXPERTS
    blk_start = jnp.arange(n_blk, dtype=jnp.int32) * blk
    blk_exp = jnp.minimum(jnp.sum(pend[None, :] <= blk_start[:, None], axis=1), N_EXPERTS - 1).astype(jnp.int32)
    slot = jnp.arange(n_blk * blk, dtype=jnp.int32)
    slot_e = jnp.repeat(blk_exp, blk)
    off = slot - pstart[slot_e]
    src = order[jnp.clip(cstart[slot_e] + off, 0, n_assign - 1)] // 2
    slot_ok = (off < counts[slot_e]) & (slot < pend[-1])
    slot_tok = jnp.where(slot_ok, src, 0)
    xb = x1[slot_tok]
    n_used = (pend[-1] // blk).astype(jnp.int32)[None]
    yb = _moe_experts(xb, blk_exp, n_used, *w_experts, layer, blk)
    dest2 = dest.reshape(t, 2)
    gates = jnp.pad(gate, ((0, 0), (0, LANES - 2)))
    return _combine(x1, yb[dest2[:, 0]], yb[dest2[:, 1]], gates, lw)


def _trunk_mixers(x, x_bf, lw, layer, mem, conv_buf, ssm_all, shift_buf, wkv_s, bsz, seq):
    tm = 512
    p_z = _proj(x_bf, lw['w_in_z'], tm, 2048)
    p_xbc = _proj(x_bf, lw['w_in_xbc'], tm, 2048)
    p_dt = _proj(x_bf, lw['w_in_dt'], tm, LANES)
    p_rwkv = _proj(x_bf, lw['w_in_rwkv'], tm, R_SHIFT_W // 2)
    p_q = _proj(x_bf, lw['w_in_q'], tm, 1024)
    p_gate = _proj(x_bf, lw['w_in_gate'], tm, 1536)

    y_a, ssm_new = _mamba(p_z, p_xbc, p_dt, conv_buf, ssm_all, layer, lw, bsz, seq)
    conv_new = p_xbc.reshape(bsz, seq, M_CONV_DIM)[:, seq - (M_CONV - 1):]
    r, w, k, v, a, kk, g_b = _rwkv_prep(p_rwkv, shift_buf, lw, bsz, seq)
    y_b, wkv_new = _wkv(r, w, k, v, a, kk, wkv_s, lw, bsz, seq)
    shift_new = p_rwkv.reshape(bsz, seq, R_SHIFT_W)[:, -1]
    y_c = _attend(p_q, mem, layer, bsz, seq)
    x1, logits = _merge(x, y_a, y_b, g_b, y_c, p_gate, lw)
    return x1, logits, conv_new, ssm_new, shift_new, wkv_new


def _layer_weights(l, w):
    w_in = w['w_in'][l]
    lw = {}
    lw['w_in_z'] = w_in[:, :OFF_XBC].astype(BF16)
    lw['w_in_xbc'] = w_in[:, OFF_XBC:OFF_DT].astype(BF16)
    lw['w_in_dt'] = jnp.pad(w_in[:, OFF_DT:OFF_RWKV], ((0, 0), (0, LANES - M_HEADS))).astype(BF16)
    lw['w_in_rwkv'] = w_in[:, OFF_RWKV:OFF_Q].astype(BF16)
    lw['w_in_q'] = w_in[:, OFF_Q:OFF_GATE].astype(BF16)
    lw['w_in_gate'] = w_in[:, OFF_GATE:].astype(BF16)
    lw['conv_w'] = w['conv_w'][l]
    lw['conv_b'] = w['conv_b'][l]
    lw['dt_bias_pad'] = jnp.pad(w['dt_bias'][l], (0, LANES - M_HEADS))[None, :]
    lw['a_log_pad'] = jnp.pad(w['a_log'][l], (0, LANES - M_HEADS))[None, :]
    lw['d_skip_t'] = jnp.broadcast_to(jnp.repeat(w['d_skip'][l], M_HEADDIM)[:, None], (M_INNER, LANES))
    lw['m_norm_w'] = w['m_norm_w'][l]
    lw['r_mu'] = w['r_mu'][l]
    lw['r_w0'] = w['r_w0'][l]
    lw['r_a0'] = w['r_a0'][l]
    zeros64 = jnp.zeros((R_LORA_WA // 2, D), F32)
    lw['r_w2_pad'] = jnp.concatenate([w['r_w2'][l], zeros64], axis=0).astype(BF16)
    lw['r_a2_pad'] = jnp.concatenate([zeros64, w['r_a2'][l]], axis=0).astype(BF16)
    lw['r_g2_bf'] = w['r_g2'][l].astype(BF16)
    lw['r_kk'] = w['r_kk'][l]
    lw['r_ka'] = w['r_ka'][l]
    lw['lnx_w_c'] = _chain_const(w['r_lnx_w'][l])
    lw['lnx_b_c'] = _chain_const(w['r_lnx_b'][l])
    lw['rk_c'] = _chain_const(w['r_rk'][l].reshape(-1))
    lw['w_mem_kv_bf'] = w['w_mem_kv'][l].astype(BF16)
    lw['w_mo_bf'] = w['w_mo'][l].astype(BF16)
    lw['w_ro_bf'] = w['w_ro'][l].astype(BF16)
    lw['w_xo_bf'] = w['w_xo'][l].astype(BF16)
    lw['w_o_bf'] = w['w_o'][l].astype(BF16)
    lw['ln1_g'] = w['ln1_g'][l]
    lw['ln1_b'] = w['ln1_b'][l]
    lw['ln2_g'] = w['ln2_g'][l]
    lw['ln2_b'] = w['ln2_b'][l]
    nr = N_EGROUPS + N_EXPERTS
    lw['w_router'] = jnp.pad(jnp.concatenate([w['w_rg'][l], w['w_re'][l]], axis=1),
                             ((0, 0), (0, LANES - nr))).astype(BF16)
    lw['b_router'] = jnp.pad(jnp.concatenate([w['b_rg'][l], w['b_re'][l]]), (0, LANES - nr))[None, :]
    return lw


def kernel(x_prompt, x_sample, mem_prompt, state_ssm, state_conv, state_wkv, state_shift, cache_mem_k, cache_mem_v, w_in, conv_w, conv_b, dt_bias, a_log, d_skip, m_norm_w, r_mu, r_w0, r_w2, r_a0, r_a2, r_g2, r_kk, r_ka, r_rk, r_lnx_w, r_lnx_b, w_mem_kv, w_mo, w_ro, w_xo, w_o, ln1_g, ln1_b, w_rg, b_rg, w_re, b_re, w_gate, w_up, w_down, ln2_g, ln2_b):
    w = dict(w_in=w_in, conv_w=conv_w, conv_b=conv_b, dt_bias=dt_bias, a_log=a_log, d_skip=d_skip,
             m_norm_w=m_norm_w, r_mu=r_mu, r_w0=r_w0, r_w2=r_w2, r_a0=r_a0, r_a2=r_a2, r_g2=r_g2, r_kk=r_kk,
             r_ka=r_ka, r_rk=r_rk, r_lnx_w=r_lnx_w, r_lnx_b=r_lnx_b, w_mem_kv=w_mem_kv, w_mo=w_mo, w_ro=w_ro,
             w_xo=w_xo, w_o=w_o, ln1_g=ln1_g, ln1_b=ln1_b, w_rg=w_rg, b_rg=b_rg, w_re=w_re, b_re=b_re,
             w_gate=w_gate, w_up=w_up, w_down=w_down, ln2_g=ln2_g, ln2_b=ln2_b)
    bp, sp, _ = x_prompt.shape
    bs, ss, _ = x_sample.shape
    n_mem = mem_prompt.shape[1]
    xp = x_prompt.reshape(bp * sp, D)
    xs = x_sample.reshape(bs * ss, D)
    mem_bf = mem_prompt.reshape(bp * n_mem, D).astype(BF16)
    xp_bf = xp.astype(BF16)
    xs_bf = xs.astype(BF16)
    w_experts = (w_gate, w_up, w_down)
    outs = {n: [] for n in ('p_ssm', 'p_conv', 'p_wkv', 'p_shift', 'p_mk', 'p_mv', 's_ssm', 's_conv', 's_wkv',
                            's_shift')}
    for l in range(DEPTH):
        lw = _layer_weights(l, w)
        mkv = _proj(mem_bf, lw['w_mem_kv_bf'], 512, 2048)
        xp1, lgp, cp, hp, shp, wp = _trunk_mixers(xp, xp_bf, lw, l, mkv, None, None, None, None, bp, sp)
        xs1, lgs, cs, hs, shs, ws = _trunk_mixers(xs, xs_bf, lw, l, (cache_mem_k, cache_mem_v), state_conv[l],
                                                  state_ssm, state_shift[l], state_wkv[l], bs, ss)
        xp, xp_bf = _hier_moe_ln(xp1, lgp, lw, w_experts, l)
        xs, xs_bf = _hier_moe_ln(xs1, lgs, lw, w_experts, l)
        outs['p_ssm'].append(hp)
        outs['p_conv'].append(cp)
        outs['p_wkv'].append(wp)
        outs['p_shift'].append(shp)
        outs['p_mk'].append(mkv[:, :D].reshape(bp, n_mem, X_HEADS, X_HEADDIM))
        outs['p_mv'].append(mkv[:, D:].reshape(bp, n_mem, X_HEADS, X_HEADDIM))
        outs['s_ssm'].append(hs)
        outs['s_conv'].append(cs)
        outs['s_wkv'].append(ws)
        outs['s_shift'].append(shs)
    st = lambda n: jnp.stack(outs[n])
    return (xp.reshape(bp, sp, D), xs.reshape(bs, ss, D), st('p_ssm'), st('p_conv'), st('p_wkv'), st('p_shift'),
            st('p_mk'), st('p_mv'), st('s_ssm'), st('s_conv'), st('s_wkv'), st('s_shift'))
```

```python
import functools

import jax
import jax.numpy as jnp
from jax import lax
from jax.experimental import pallas as pl
from jax.experimental.pallas import tpu as pltpu

F32 = jnp.float32
BF16 = jnp.bfloat16
HIGHEST = lax.Precision.HIGHEST

LANES = 128
SUBLANES = 8
VMEM_LIMIT = 56 * 1024 * 1024

D = 1024
DEPTH = 4
M_INNER = 2048
M_HEADDIM = 64
M_HEADS = 32
M_GROUPS = 8
M_HPG = M_HEADS // M_GROUPS
M_STATE = 128
M_CONV = 4
M_CONV_DIM = M_INNER + 2 * M_GROUPS * M_STATE
M_CHUNK = 128
RMS_EPS = 1e-5
R_HEADS = 16
R_HS = 64
R_LORA_WA = 128
R_LORA_G = 128
R_SHIFT_W = 3 * D + R_LORA_WA + R_LORA_G
R_GN_EPS = 64e-5
MEM_LEN = 256
X_HEADS = 4
X_HEADDIM = 256
OFF_XBC = M_INNER
OFF_DT = OFF_XBC + M_CONV_DIM
OFF_RWKV = OFF_DT + M_HEADS
OFF_Q = OFF_RWKV + R_SHIFT_W
OFF_GATE = OFF_Q + D
N_EGROUPS = 4
E_PER_GROUP = 8
N_EXPERTS = 32
D_EXPERT = 512
LN_EPS = 1e-5
ALPHA = (2 * DEPTH) ** 0.25


def _params(*sem):
    return pltpu.CompilerParams(dimension_semantics=sem, vmem_limit_bytes=VMEM_LIMIT)


def _sigmoid(x):
    return 1.0 / (1.0 + jnp.exp(-x))


def _silu(x):
    return x * _sigmoid(x)


def _softplus(x):
    return jnp.maximum(x, 0.0) + jnp.log1p(jnp.exp(-jnp.abs(x)))


def _layer_norm(h, g, b):
    mu = jnp.mean(h, axis=-1, keepdims=True)
    d = h - mu
    var = jnp.mean(d * d, axis=-1, keepdims=True)
    return d * lax.rsqrt(var + LN_EPS) * g + b


def _proj_kernel(x_ref, w_ref, o_ref):
    o_ref[...] = jnp.dot(x_ref[...], w_ref[...], preferred_element_type=F32)


def _proj(x, w, tm, tn):
    m, k = x.shape
    n = w.shape[1]
    tm = min(tm, m)
    tn = min(tn, n)
    return pl.pallas_call(
        _proj_kernel,
        grid=(n // tn, m // tm),
        in_specs=[pl.BlockSpec((tm, k), lambda j, i: (i, 0)),
                  pl.BlockSpec((k, tn), lambda j, i: (0, j))],
        out_specs=pl.BlockSpec((tm, tn), lambda j, i: (i, j)),
        out_shape=jax.ShapeDtypeStruct((m, n), F32),
        compiler_params=_params("arbitrary", "arbitrary"),
        name="proj",
    )(x, w)


def _mamba_kernel(z_ref, xbc_ref, dt_ref, cbuf_ref, h0_ref, convw_ref, convb_ref, dtb_ref, alog_ref,
                  dskip_ref, mnw_ref, y_ref, hT_ref, ubuf, h_scr, yT_scr, *, rows, nc, q):
    c = pl.program_id(1)

    @pl.when(c == 0)
    def _():
        if rows < q:
            ubuf[...] = jnp.zeros(ubuf.shape, F32)
        ubuf[0:SUBLANES, :] = cbuf_ref[0]
        h_scr[...] = h0_ref[0, 0]

    ubuf[SUBLANES:SUBLANES + rows, :] = xbc_ref[...]
    cw = convw_ref[...]
    acc = ubuf[5:5 + q, :] * cw[0:1]
    acc = acc + ubuf[6:6 + q, :] * cw[1:2]
    acc = acc + ubuf[7:7 + q, :] * cw[2:3]
    acc = acc + ubuf[8:8 + q, :] * cw[3:4]
    acc = acc + convb_ref[...]
    if nc > 1:
        ubuf[0:SUBLANES, :] = ubuf[q:q + SUBLANES, :]
    act = _silu(acc)

    row_id = lax.broadcasted_iota(jnp.int32, (q, q), 0)
    col_id = lax.broadcasted_iota(jnp.int32, (q, q), 1)
    dt_raw = dt_ref[...]
    if rows < q:
        dt_raw = jnp.concatenate([dt_raw, jnp.zeros((q - rows, LANES), F32)], axis=0)
    dt = _softplus(dt_raw + dtb_ref[...])
    a_neg = -jnp.exp(alog_ref[...])
    adt = a_neg * dt
    if rows < q:
        valid = lax.broadcasted_iota(jnp.int32, (q, 1), 0) < rows
        act = jnp.where(valid, act, 0.0)
        adt = jnp.where(valid, adt, 0.0)
    tril = (row_id >= col_id).astype(F32)
    acs = jnp.dot(tril, adt, precision=HIGHEST, preferred_element_type=F32)
    acs_t = acs.T
    dt_t = dt.T
    xs_t = act[:, :M_INNER].T
    upper = row_id <= col_id

    for g in range(M_GROUPS):
        b_g = act[:, M_INNER + g * M_STATE:M_INNER + (g + 1) * M_STATE]
        c_g = act[:, M_INNER + (M_GROUPS + g) * M_STATE:M_INNER + (M_GROUPS + g + 1) * M_STATE]
        b_bf = b_g.astype(BF16)
        c_t = c_g.T
        c_t_bf = c_t.astype(BF16)
        cb_t = jnp.dot(b_bf, c_t_bf, preferred_element_type=F32)
        for r in range(M_HPG):
            h = g * M_HPG + r
            a_row = acs_t[h:h + 1, :]
            a_col = acs[:, h:h + 1]
            lm_t = jnp.exp(jnp.where(upper, a_row - a_col, -jnp.inf))
            m_t = (cb_t * lm_t).astype(BF16)
            cs_t = (c_t * jnp.exp(a_row)).astype(BF16)
            x_t = xs_t[h * M_HEADDIM:(h + 1) * M_HEADDIM, :]
            xdt_t = x_t * dt_t[h:h + 1, :]
            hprev = h_scr[h]
            y_t = (jnp.dot(xdt_t.astype(BF16), m_t, preferred_element_type=F32)
                   + jnp.dot(hprev.astype(BF16), cs_t, preferred_element_type=F32))
            yT_scr[h * M_HEADDIM:(h + 1) * M_HEADDIM, :] = y_t + x_t * dskip_ref[h * M_HEADDIM:(h + 1) * M_HEADDIM, 0:q]
            a_last = acs[q - 1:q, h:h + 1]
            wrow = jnp.exp(a_last - a_row)
            s_c = jnp.dot((xdt_t * wrow).astype(BF16), b_bf, preferred_element_type=F32)
            h_scr[h] = hprev * jnp.exp(a_last) + s_c

    y = yT_scr[...].T
    if rows < q:
        y = y[0:rows, :]
    y = y * _silu(z_ref[...])
    gw = M_INNER // M_GROUPS
    parts = []
    for g in range(M_GROUPS):
        yg = y[:, g * gw:(g + 1) * gw]
        ms = jnp.mean(yg * yg, axis=-1, keepdims=True)
        parts.append(yg * lax.rsqrt(ms + RMS_EPS))
    y_ref[...] = jnp.concatenate(parts, axis=-1) * mnw_ref[...]

    @pl.when(c == nc - 1)
    def _():
        hT_ref[0] = h_scr[...]


def _mamba(p_z, p_xbc, p_dt, conv_buf, h0_all, layer, lw, bsz, seq, row0):
    if seq >= M_CHUNK:
        q = rows = M_CHUNK
        nc = seq // q
    else:
        q = rows = seq
        nc = 1
    if conv_buf is None:
        cbuf = jnp.zeros((1, SUBLANES, M_CONV_DIM), F32)
        cbuf_map = lambda b, c: (0, 0, 0)
    else:
        cbuf = jnp.pad(conv_buf, ((0, 0), (SUBLANES - (M_CONV - 1), 0), (0, 0)))
        cbuf_map = lambda b, c: (b, 0, 0)
    if h0_all is None:
        h0_all = jnp.zeros((1, 1, M_HEADS, M_HEADDIM, M_STATE), F32)
        h0_map = lambda b, c: (0, 0, 0, 0, 0)
    else:
        h0_map = lambda b, c: (layer, b, 0, 0, 0)
    convw = jnp.pad(lw['conv_w'], ((0, SUBLANES - M_CONV), (0, 0)))
    kern = functools.partial(_mamba_kernel, rows=rows, nc=nc, q=q)
    row_map = lambda b, c: (b * nc + c, 0)
    in_map = lambda b, c: (row0 // rows + b * nc + c, 0)
    const2 = lambda b, c: (0, 0)
    y, h_last = pl.pallas_call(
        kern,
        grid=(bsz, nc),
        in_specs=[pl.BlockSpec((rows, M_INNER), in_map),
                  pl.BlockSpec((rows, M_CONV_DIM), in_map),
                  pl.BlockSpec((rows, LANES), in_map),
                  pl.BlockSpec((1, SUBLANES, M_CONV_DIM), cbuf_map),
                  pl.BlockSpec((1, 1, M_HEADS, M_HEADDIM, M_STATE), h0_map),
                  pl.BlockSpec((SUBLANES, M_CONV_DIM), const2),
                  pl.BlockSpec((1, M_CONV_DIM), const2),
                  pl.BlockSpec((1, LANES), const2),
                  pl.BlockSpec((1, LANES), const2),
                  pl.BlockSpec((M_INNER, LANES), const2),
                  pl.BlockSpec((1, M_INNER), const2)],
        out_specs=[pl.BlockSpec((rows, M_INNER), row_map),
                   pl.BlockSpec((1, M_HEADS, M_HEADDIM, M_STATE), lambda b, c: (b, 0, 0, 0))],
        out_shape=[jax.ShapeDtypeStruct((bsz * seq, M_INNER), F32),
                   jax.ShapeDtypeStruct((bsz, M_HEADS, M_HEADDIM, M_STATE), F32)],
        scratch_shapes=[pltpu.VMEM((q + 2 * SUBLANES, M_CONV_DIM), F32),
                        pltpu.VMEM((M_HEADS, M_HEADDIM, M_STATE), F32),
                        pltpu.VMEM((M_INNER, q), F32)],
        compiler_params=_params("arbitrary", "arbitrary"),
        name="mamba_ssd",
    )(p_z, p_xbc, p_dt, cbuf, h0_all, convw, lw['conv_b'][None, :], lw['dt_bias_pad'], lw['a_log_pad'],
      lw['d_skip_t'], lw['m_norm_w'][None, :])
    return y, h_last


def _rwkv_prep_kernel(cols_ref, shift_ref, mu_ref, w0_ref, w2_ref, a0_ref, a2_ref, g2_ref, kkw_ref, kaw_ref,
                      r_out, w_out, k_out, v_out, a_out, kk_out, g_out, sbuf, *, tl, nt):
    t = pl.program_id(1)

    @pl.when(t == 0)
    def _():
        sbuf[0:SUBLANES, :] = shift_ref[0]

    cols = cols_ref[...]
    sbuf[SUBLANES:SUBLANES + tl, :] = cols
    prev = sbuf[SUBLANES - 1:SUBLANES - 1 + tl, :]
    if nt > 1:
        sbuf[0:SUBLANES, :] = sbuf[tl:tl + SUBLANES, :]
    mixed = cols + (prev - cols) * mu_ref[...]
    r = mixed[:, 0:D]
    k = mixed[:, D:2 * D]
    v = mixed[:, 2 * D:3 * D]
    wa = mixed[:, 3 * D:3 * D + R_LORA_WA]
    gl = mixed[:, 3 * D + R_LORA_WA:]
    lw_ = jnp.dot(jnp.tanh(wa).astype(BF16), w2_ref[...], preferred_element_type=F32)
    la_ = jnp.dot(wa.astype(BF16), a2_ref[...], preferred_element_type=F32)
    g = jnp.dot(_sigmoid(gl).astype(BF16), g2_ref[...], preferred_element_type=F32)
    w_log = -_softplus(-(w0_ref[...] + lw_)) - 0.5
    decay = jnp.exp(-jnp.exp(w_log))
    a = _sigmoid(a0_ref[...] + la_)
    r_out[...] = r
    w_out[...] = decay
    k_out[...] = k * (1.0 + (a - 1.0) * kaw_ref[...])
    v_out[...] = v
    a_out[...] = a
    kk_out[...] = k * kkw_ref[...]
    g_out[...] = g


def _rwkv_prep(p_rwkv, shift_buf, lw, bsz, seq, row0):
    tl = min(seq, 256)
    nt = seq // tl
    if shift_buf is None:
        sb = jnp.zeros((1, SUBLANES, R_SHIFT_W), F32)
        sb_map = lambda b, t: (0, 0, 0)
    else:
        sb = jnp.pad(shift_buf[:, None, :], ((0, 0), (SUBLANES - 1, 0), (0, 0)))
        sb_map = lambda b, t: (b, 0, 0)
    row_map = lambda b, t: (b * nt + t, 0)
    const2 = lambda b, t: (0, 0)
    vec = pl.BlockSpec((1, D), const2)
    outs = pl.pallas_call(
        functools.partial(_rwkv_prep_kernel, tl=tl, nt=nt),
        grid=(bsz, nt),
        in_specs=[pl.BlockSpec((tl, R_SHIFT_W), lambda b, t: (row0 // tl + b * nt + t, 0)),
                  pl.BlockSpec((1, SUBLANES, R_SHIFT_W), sb_map),
                  pl.BlockSpec((1, R_SHIFT_W), const2),
                  vec, pl.BlockSpec((R_LORA_WA, D), const2),
                  vec, pl.BlockSpec((R_LORA_WA, D), const2),
                  pl.BlockSpec((R_LORA_G, D), const2),
                  vec, vec],
        out_specs=[pl.BlockSpec((tl, D), row_map)] * 7,
        out_shape=[jax.ShapeDtypeStruct((bsz * seq, D), F32)] * 7,
        scratch_shapes=[pltpu.VMEM((tl + 2 * SUBLANES, R_SHIFT_W), F32)],
        compiler_params=_params("arbitrary", "arbitrary"),
        name="rwkv_prep",
    )(p_rwkv, sb, lw['r_mu'][None, :], lw['r_w0'][None, :], lw['r_w2_pad'], lw['r_a0'][None, :], lw['r_a2_pad'],
      lw['r_g2_bf'], lw['r_kk'][None, :], lw['r_ka'][None, :])
    return outs


CH_B = LANES // R_HEADS
CH_HALF = LANES // 2


def _wkv_kernel(r_ref, w_ref, k_ref, v_ref, a_ref, kk_ref, s0_ref, lnw_ref, lnb_ref, rk_ref,
                y_ref, sT_ref, s_scr, ch_scr, m_scr, tok_scr, yflat_scr, *, tb, nb):
    ng = R_HS // SUBLANES
    nhh = R_HEADS // 2
    tblk = pl.program_id(1)
    low_half = lax.broadcasted_iota(jnp.int32, (SUBLANES, LANES), 1) < CH_HALF

    @pl.when(tblk == 0)
    def _():
        s_scr[...] = s0_ref[0]

    sub_id = lax.broadcasted_iota(jnp.int32, (SUBLANES, LANES), 0)

    def swap_rows(vs):
        for s in (1, 2, 4):
            keep_low = (sub_id & s) == 0
            nxt_vs = []
            for i in range(SUBLANES):
                other = vs[i ^ s]
                if i & s:
                    nxt_vs.append(jnp.where(keep_low, pltpu.roll(other, SUBLANES - s, axis=0), vs[i]))
                else:
                    nxt_vs.append(jnp.where(keep_low, vs[i], pltpu.roll(other, s, axis=0)))
            vs = nxt_vs
        return vs

    for idx, ref in enumerate((r_ref, w_ref, k_ref, v_ref, a_ref, kk_ref)):
        def regroup(hh, carry, idx=idx, ref=ref):
            lanes = pl.ds(pl.multiple_of(hh * LANES, LANES), LANES)
            for tg in range(tb // SUBLANES):
                tiles = swap_rows([ref[b, tg * SUBLANES:(tg + 1) * SUBLANES, lanes] for b in range(CH_B)])
                for tt in range(SUBLANES):
                    tok_scr[idx, hh, tg * SUBLANES + tt] = tiles[tt]
            return carry

        lax.fori_loop(0, nhh, regroup, 0)

    def rows_at(idx, t, hh):
        return tok_scr[idx, hh, t]

    def stage_rows(t):
        for p in range(3):
            for hh in range(nhh):
                av, bv = rows_at(2 * p, t, hh), rows_at(2 * p + 1, t, hh)
                m_scr[p, hh * CH_B:(hh + 1) * CH_B, :] = jnp.where(low_half, av, pltpu.roll(bv, CH_HALF, axis=1))
                m_scr[p, CH_HALF + hh * CH_B:CH_HALF + (hh + 1) * CH_B, :] = jnp.where(
                    low_half, pltpu.roll(av, CH_HALF, axis=1), bv)

    def stage_chain(slot):
        rw, kv, akk = m_scr[0].T, m_scr[1].T, m_scr[2].T
        kkr = akk[R_HS:]
        nrm = jnp.sqrt(jnp.sum(kkr * kkr, axis=0, keepdims=True))
        kk = kkr / jnp.maximum(nrm, 1e-12)
        for idx, val in enumerate((rw[:R_HS], rw[R_HS:], kv[:R_HS], kv[R_HS:], kk * akk[:R_HS], -kk)):
            ch_scr[slot, idx] = val

    def row(ref_view, j):
        return jnp.broadcast_to(ref_view[pl.ds(j, 1), :], (SUBLANES, LANES))

    zeros = tuple(jnp.zeros((SUBLANES, LANES), F32) for _ in range(ng))
    stage_rows(0)
    stage_chain(0)
    stage_rows(min(1, tb - 1))
    stage_chain(1)
    stage_rows(min(2, tb - 1))

    def first_sa(j, acc):
        aj = row(ch_scr.at[0, 5], j)
        return tuple(acc[ig] + s_scr[ig, j] * aj for ig in range(ng))

    sa0 = lax.fori_loop(0, R_HS, first_sa, zeros)

    def step(t, sa):
        cur = t % 3
        nxt = (t + 1) % 3
        v = ch_scr[cur, 3]
        vs = tuple(v[ig * SUBLANES:(ig + 1) * SUBLANES, :] for ig in range(ng))
        r_t, w_t, k_t, b_t, a_next = (ch_scr.at[cur, 0], ch_scr.at[cur, 1], ch_scr.at[cur, 2], ch_scr.at[cur, 4],
                                      ch_scr.at[nxt, 5])

        def col(j, carry):
            yacc, san = carry
            wj, bj, kj, rj, aj = row(w_t, j), row(b_t, j), row(k_t, j), row(r_t, j), row(a_next, j)
            ynew, snew = [], []
            for ig in range(ng):
                s = s_scr[ig, j] * wj + sa[ig] * bj + vs[ig] * kj
                s_scr[ig, j] = s
                ynew.append(yacc[ig] + s * rj)
                snew.append(san[ig] + s * aj)
            return tuple(ynew), tuple(snew)

        yacc, san = lax.fori_loop(0, R_HS, col, (zeros, zeros), unroll=8)
        y = jnp.concatenate(yacc, axis=0)
        mu = jnp.mean(y, axis=0, keepdims=True)
        d = y - mu
        var = jnp.mean(d * d, axis=0, keepdims=True)
        yn = d * lax.rsqrt(var + R_GN_EPS) * lnw_ref[...] + lnb_ref[...]
        bonus = jnp.sum(ch_scr[cur, 0] * ch_scr[cur, 2] * rk_ref[...], axis=0, keepdims=True) * v
        yo = yn + bonus
        yt = jnp.concatenate([yo, yo], axis=0).T
        for hh in range(nhh):
            even = yt[hh * CH_B:(hh + 1) * CH_B]
            odd = yt[CH_HALF + hh * CH_B:CH_HALF + (hh + 1) * CH_B]
            yflat_scr[hh, t] = jnp.where(low_half, even, odd)
        stage_chain((t + 2) % 3)
        stage_rows(jnp.minimum(t + 3, tb - 1))
        return tuple(san)

    lax.fori_loop(0, tb, step, sa0)

    def ungroup(hh, carry):
        lanes = pl.ds(pl.multiple_of(hh * LANES, LANES), LANES)
        for tg in range(tb // SUBLANES):
            tiles = swap_rows([yflat_scr[hh, tg * SUBLANES + tt] for tt in range(SUBLANES)])
            for b in range(CH_B):
                y_ref[b, tg * SUBLANES:(tg + 1) * SUBLANES, lanes] = tiles[b]
        return carry

    lax.fori_loop(0, nhh, ungroup, 0)

    @pl.when(tblk == nb - 1)
    def _():
        sT_ref[0] = s_scr[...]


def _chain_const(t):
    th = t.reshape(R_HEADS // 2, 2, R_HS).transpose(2, 1, 0)
    return jnp.broadcast_to(th[..., None], (R_HS, 2, R_HEADS // 2, CH_B)).reshape(R_HS, LANES)


def _wkv(r, w, k, v, a, kk, s0, lw, bsz, seq):
    g = bsz // CH_B
    ng = R_HS // SUBLANES
    nhh = R_HEADS // 2
    tb = min(seq, 32)
    nb = seq // tb
    tok =[t.reshape(bsz, seq, D) for t in (r, w, k, v, a, kk)]
    st_shape = (ng, R_HS, SUBLANES, LANES)
    if s0 is None:
        s0c = jnp.zeros((1,) + st_shape, F32)
        s_in = pl.BlockSpec((1,) + st_shape, lambda gi, ti: (0, 0, 0, 0, 0))
    else:
        s0c = s0.reshape(g, CH_B, nhh, 2, ng, SUBLANES, R_HS).transpose(0, 4, 6, 5, 3, 2, 1).reshape((g,) + st_shape)
        s_in = pl.BlockSpec((1,) + st_shape, lambda gi, ti: (gi, 0, 0, 0, 0))
    blk = pl.BlockSpec((CH_B, tb, D), lambda gi, ti: (gi, ti, 0))
    s_out_spec = pl.BlockSpec((1,) + st_shape, lambda gi, ti: (gi, 0, 0, 0, 0))
    cblk = pl.BlockSpec((R_HS, LANES), lambda gi, ti: (0, 0))
    y, s_last = pl.pallas_call(
        functools.partial(_wkv_kernel, tb=tb, nb=nb),
        grid=(g, nb),
        in_specs=[blk] * 6 + [s_in, cblk, cblk, cblk],
        out_specs=[blk, s_out_spec],
        out_shape=[jax.ShapeDtypeStruct((bsz, seq, D), F32),
                   jax.ShapeDtypeStruct((g,) + st_shape, F32)],
        scratch_shapes=[pltpu.VMEM(st_shape, F32),
                        pltpu.VMEM((3, 6, R_HS, LANES), F32),
                        pltpu.VMEM((3, LANES, LANES), F32),
                        pltpu.VMEM((6, nhh, tb, CH_B, LANES), F32),
                        pltpu.VMEM((nhh, tb, CH_B, LANES), F32)],
        compiler_params=_params("arbitrary", "arbitrary"),
        name="wkv7",
    )(*tok, s0c, lw['lnx_w_c'], lw['lnx_b_c'], lw['rk_c'])
    s_out = s_last.reshape(g, ng, R_HS, SUBLANES, 2, nhh, CH_B).transpose(0, 6, 5, 4, 1, 3, 2)
    return y.reshape(bsz * seq, D), s_out.reshape(bsz, R_HEADS, R_HS, R_HS)


def _attn_kernel(q_ref, k_ref, v_ref, o_ref, *, head_major_cols):
    q = q_ref[...]
    scale = X_HEADDIM ** -0.5
    outs = []
    if not head_major_cols:
        tl = q.shape[0]
        n_rows = MEM_LEN * X_HEADS
        k_all = k_ref[0, 0].reshape(n_rows, X_HEADDIM).astype(BF16)
        v_all = v_ref[0, 0].reshape(n_rows, X_HEADDIM).astype(BF16)
        q_all = jnp.concatenate([q[:, h * X_HEADDIM:(h + 1) * X_HEADDIM] for h in range(X_HEADS)], axis=0)
        s = lax.dot_general(q_all.astype(BF16), k_all, (((1,), (1,)), ((), ())), preferred_element_type=F32) * scale
        q_head = lax.broadcasted_iota(jnp.int32, (X_HEADS * tl, n_rows), 0) // tl
        kv_head = lax.broadcasted_iota(jnp.int32, (X_HEADS * tl, n_rows), 1) % X_HEADS
        s = jnp.where(q_head == kv_head, s, -jnp.inf)
        s = s - jnp.max(s, axis=-1, keepdims=True)
        e = jnp.exp(s)
        p = e / jnp.sum(e, axis=-1, keepdims=True)
        o_all = jnp.dot(p.astype(BF16), v_all, preferred_element_type=F32)
        o_ref[...] = jnp.concatenate([o_all[h * tl:(h + 1) * tl] for h in range(X_HEADS)], axis=-1)
        return
    for h in range(X_HEADS):
        sl = slice(h * X_HEADDIM, (h + 1) * X_HEADDIM)
        qh = q[:, sl].astype(BF16)
        kh = k_ref[:, sl].astype(BF16)
        vh = v_ref[:, sl].astype(BF16)
        s = lax.dot_general(qh, kh, (((1,), (1,)), ((), ())), preferred_element_type=F32) * scale
        s = s - jnp.max(s, axis=-1, keepdims=True)
        e = jnp.exp(s)
        p = e / jnp.sum(e, axis=-1, keepdims=True)
        outs.append(jnp.dot(p.astype(BF16), vh, preferred_element_type=F32))
    o_ref[...] = jnp.concatenate(outs, axis=-1)


def _attend(p_q, mem, layer, bsz, seq, row0):
    tl = min(seq, 512)
    nt = seq // tl
    if isinstance(mem, tuple):
        cache_k, cache_v = mem
        cblk = pl.BlockSpec((1, 1, MEM_LEN, X_HEADS, X_HEADDIM), lambda b, t: (layer, b, 0, 0, 0))
        kv_specs, kv_args, head_major_cols = [cblk, cblk], (cache_k, cache_v), False
    else:
        kv_specs = [pl.BlockSpec((MEM_LEN, D), lambda b, t: (b, 0)), pl.BlockSpec((MEM_LEN, D), lambda b, t: (b, 1))]
        kv_args, head_major_cols = (mem, mem), True
    return pl.pallas_call(
        functools.partial(_attn_kernel, head_major_cols=head_major_cols),
        grid=(bsz, nt),
        in_specs=[pl.BlockSpec((tl, D), lambda b, t: (row0 // tl + b * nt + t, 0))] + kv_specs,
        out_specs=pl.BlockSpec((tl, D), lambda b, t: (b * nt + t, 0)),
        out_shape=jax.ShapeDtypeStruct((bsz * seq, D), F32),
        compiler_params=_params("arbitrary", "arbitrary"),
        name="mem_attn",
    )(p_q, *kv_args)


def _merge_kernel(x_ref, ya_ref, yb_ref, g_ref, yc_ref, gate_ref, wmo_ref, wro_ref, wxo_ref, wo_ref,
                  lng_ref, lnb_ref, wr_ref, br_ref, x1_ref, lg_ref):
    gate = gate_ref[...]
    ma = jnp.dot(ya_ref[...].astype(BF16), wmo_ref[...], preferred_element_type=F32)
    mb = jnp.dot((yb_ref[...] * g_ref[...]).astype(BF16), wro_ref[...], preferred_element_type=F32)
    mc = jnp.dot(yc_ref[...].astype(BF16), wxo_ref[...], preferred_element_type=F32)
    merged = (_sigmoid(gate[:, 0:D]) * ma + _sigmoid(gate[:, D:2 * D]) * mb) + _sigmoid(gate[:, 2 * D:3 * D]) * mc
    h = ALPHA * x_ref[...] + jnp.dot(merged.astype(BF16), wo_ref[...], preferred_element_type=F32)
    x1 = _layer_norm(h, lng_ref[...], lnb_ref[...])
    x1_ref[...] = x1
    lg_ref[...] = jnp.dot(x1.astype(BF16), wr_ref[...], preferred_element_type=F32) + br_ref[...]


def _merge(x, y_a, y_b, g_b, y_c, p_gate, lw, row0):
    m = y_a.shape[0]
    tm = min(m, 256)
    row = lambda w: pl.BlockSpec((tm, w), lambda i: (i, 0))
    row_all = lambda w: pl.BlockSpec((tm, w), lambda i: (row0 // tm + i, 0))
    full = lambda a, b: pl.BlockSpec((a, b), lambda i: (0, 0))
    return pl.pallas_call(
        _merge_kernel,
        grid=(m // tm,),
        in_specs=[row_all(D), row(M_INNER), row(D), row(D), row(D), row_all(3 * D),
                  full(M_INNER, D), full(D, D), full(D, D), full(D, D),
                  full(1, D), full(1, D), full(D, LANES), full(1, LANES)],
        out_specs=[row(D), row(LANES)],
        out_shape=[jax.ShapeDtypeStruct((m, D), F32), jax.ShapeDtypeStruct((m, LANES), F32)],
        compiler_params=_params("arbitrary"),
        name="merge_ln_router",
    )(x, y_a, y_b, g_b, y_c, p_gate, lw['w_mo_bf'], lw['w_ro_bf'], lw['w_xo_bf'], lw['w_o_bf'],
      lw['ln1_g'][None, :], lw['ln1_b'][None, :], lw['w_router'], lw['b_router'])


def _moe_kernel(be_ref, nu_ref, x_ref, wg_ref, wu_ref, wd_ref, o_ref, wg_bf, wu_bf, wd_bf):
    i = pl.program_id(0)
    active = i < nu_ref[0]
    new_expert = jnp.logical_or(i == 0, be_ref[i] != be_ref[jnp.maximum(i - 1, 0)])

    @pl.when(jnp.logical_and(active, new_expert))
    def _():
        wg_bf[...] = wg_ref[0, 0].astype(BF16)
        wu_bf[...] = wu_ref[0, 0].astype(BF16)
        wd_bf[...] = wd_ref[0, 0].astype(BF16)

    @pl.when(active)
    def _():
        xb = x_ref[...].astype(BF16)
        hid = _silu(jnp.dot(xb, wg_bf[...], preferred_element_type=F32)) * jnp.dot(xb, wu_bf[...],
                                                                                    preferred_element_type=F32)
        o_ref[...] = jnp.dot(hid.astype(BF16), wd_bf[...], preferred_element_type=F32)

    @pl.when(jnp.logical_not(active))
    def _():
        o_ref[...] = jnp.zeros(o_ref.shape, F32)


def _moe_experts(xb, blk_exp, n_used, w_gate, w_up, w_down, layer, blk):
    n_blk = xb.shape[0] // blk
    grid_spec = pltpu.PrefetchScalarGridSpec(
        num_scalar_prefetch=2,
        grid=(n_blk,),
        in_specs=[pl.BlockSpec((blk, D), lambda i, be, nu: (i, 0)),
                  pl.BlockSpec((1, 1, D, D_EXPERT), lambda i, be, nu: (layer, be[i], 0, 0)),
                  pl.BlockSpec((1, 1, D, D_EXPERT), lambda i, be, nu: (layer, be[i], 0, 0)),
                  pl.BlockSpec((1, 1, D_EXPERT, D), lambda i, be, nu: (layer, be[i], 0, 0))],
        out_specs=pl.BlockSpec((blk, D), lambda i, be, nu: (i, 0)),
        scratch_shapes=[pltpu.VMEM((D, D_EXPERT), BF16), pltpu.VMEM((D, D_EXPERT), BF16),
                        pltpu.VMEM((D_EXPERT, D), BF16)],
    )
    return pl.pallas_call(
        _moe_kernel,
        grid_spec=grid_spec,
        out_shape=jax.ShapeDtypeStruct((n_blk * blk, D), F32),
        compiler_params=_params("arbitrary"),
        name="moe_experts",
    )(blk_exp, n_used, xb, w_gate, w_up, w_down)


def _combine_kernel(x_ref, y0_ref, y1_ref, gt_ref, lng_ref, lnb_ref, o_ref, ob_ref):
    gt = gt_ref[...]
    moe = gt[:, 0:1] * y0_ref[...] + gt[:, 1:2] * y1_ref[...]
    x2 = _layer_norm(ALPHA * x_ref[...] + moe, lng_ref[...], lnb_ref[...])
    o_ref[...] = x2
    ob_ref[...] = x2.astype(BF16)


def _combine(x1, y0, y1, gates, lw):
    m = x1.shape[0]
    tm = min(m, 512)
    row = lambda w: pl.BlockSpec((tm, w), lambda i: (i, 0))
    full = lambda a, b: pl.BlockSpec((a, b), lambda i: (0, 0))
    return pl.pallas_call(
        _combine_kernel,
        grid=(m // tm,),
        in_specs=[row(D), row(D), row(D), row(LANES), full(1, D), full(1, D)],
        out_specs=[row(D), row(D)],
        out_shape=[jax.ShapeDtypeStruct((m, D), F32), jax.ShapeDtypeStruct((m, D), BF16)],
        compiler_params=_params("arbitrary"),
        name="moe_combine_ln",
    )(x1, y0, y1, gates, lw['ln2_g'][None, :], lw['ln2_b'][None, :])


def _hier_moe_ln(x1, logits, lw, w_experts, layer):
    t = x1.shape[0]
    blk = 256 if t >= 8192 else 128
    lg = logits[:, :N_EGROUPS]
    le = logits[:, N_EGROUPS:N_EGROUPS + N_EXPERTS].reshape(t, N_EGROUPS, E_PER_GROUP)
    g_sel = jnp.argmax(lg, axis=-1).astype(jnp.int32)
    g_prob = jnp.take_along_axis(jax.nn.softmax(lg, axis=-1), g_sel[:, None], axis=-1)
    le = jnp.take_along_axis(le, g_sel[:, None, None], axis=1)[:, 0]
    top_v, top_i = lax.top_k(le, 2)
    gate = g_prob * jax.nn.softmax(top_v, axis=-1)
    flat_e = (g_sel[:, None] * E_PER_GROUP + top_i.astype(jnp.int32)).reshape(-1)
    n_assign = 2 * t
    order = jnp.argsort(flat_e).astype(jnp.int32)
    onehot = (flat_e[:, None] == jnp.arange(N_EXPERTS, dtype=jnp.int32)[None, :]).astype(jnp.int32)
    seen = jnp.cumsum(onehot, axis=0)
    counts = seen[-1]
    rank = jnp.sum(seen * onehot, axis=1) - 1
    padded = (counts + blk - 1) // blk * blk
    pend = jnp.cumsum(padded)
    pstart = pend - padded
    cstart = jnp.cumsum(counts) - counts
    dest = pstart[flat_e] + rank
    n_blk = n_assign // blk + N_EXPERTS
    blk_start = jnp.arange(n_blk, dtype=jnp.int32) * blk
    blk_exp = jnp.minimum(jnp.sum(pend[None, :] <= blk_start[:, None], axis=1), N_EXPERTS - 1).astype(jnp.int32)
    slot = jnp.arange(n_blk * blk, dtype=jnp.int32)
    slot_e = jnp.repeat(blk_exp, blk)
    off = slot - pstart[slot_e]
    src = order[jnp.clip(cstart[slot_e] + off, 0, n_assign - 1)] // 2
    slot_ok = (off < counts[slot_e]) & (slot < pend[-1])
    slot_tok = jnp.where(slot_ok, src, 0)
    xb = x1[slot_tok]
    n_used = (pend[-1] // blk).astype(jnp.int32)[None]
    yb = _moe_experts(xb, blk_exp, n_used, *w_experts, layer, blk)
    dest2 = dest.reshape(t, 2)
    gates = jnp.pad(gate, ((0, 0), (0, LANES - 2)))
    return _combine(x1, yb[dest2[:, 0]], yb[dest2[:, 1]], gates, lw)


def _in_proj(x_bf, lw):
    tm = 512
    return dict(z=_proj(x_bf, lw['w_in_z'], tm, 2048),
                xbc=_proj(x_bf, lw['w_in_xbc'], tm, 2048),
                dt=_proj(x_bf, lw['w_in_dt'], tm, LANES),
                rwkv=_proj(x_bf, lw['w_in_rwkv'], tm, R_SHIFT_W // 2),
                q=_proj(x_bf, lw['w_in_q'], tm, 1024),
                gate=_proj(x_bf, lw['w_in_gate'], tm, 1536))


def _trunk_mixers(x, p, lw, layer, mem, conv_buf, ssm_all, shift_buf, wkv_s, bsz, seq, row0):
    n = bsz * seq
    y_a, ssm_new = _mamba(p['z'], p['xbc'], p['dt'], conv_buf, ssm_all, layer, lw, bsz, seq, row0)
    conv_new = p['xbc'][row0:row0 + n].reshape(bsz, seq, M_CONV_DIM)[:, seq - (M_CONV - 1):]
    r, w, k, v, a, kk, g_b = _rwkv_prep(p['rwkv'], shift_buf, lw, bsz, seq, row0)
    y_b, wkv_new = _wkv(r, w, k, v, a, kk, wkv_s, lw, bsz, seq)
    shift_new = p['rwkv'][row0:row0 + n].reshape(bsz, seq, R_SHIFT_W)[:, -1]
    y_c = _attend(p['q'], mem, layer, bsz, seq, row0)
    x1, logits = _merge(x, y_a, y_b, g_b, y_c, p['gate'], lw, row0)
    return x1, logits, conv_new, ssm_new, shift_new, wkv_new


def _layer_weights(l, w):
    w_in = w['w_in'][l]
    lw = {}
    lw['w_in_z'] = w_in[:, :OFF_XBC].astype(BF16)
    lw['w_in_xbc'] = w_in[:, OFF_XBC:OFF_DT].astype(BF16)
    lw['w_in_dt'] = jnp.pad(w_in[:, OFF_DT:OFF_RWKV], ((0, 0), (0, LANES - M_HEADS))).astype(BF16)
    lw['w_in_rwkv'] = w_in[:, OFF_RWKV:OFF_Q].astype(BF16)
    lw['w_in_q'] = w_in[:, OFF_Q:OFF_GATE].astype(BF16)
    lw['w_in_gate'] = w_in[:, OFF_GATE:].astype(BF16)
    lw['conv_w'] = w['conv_w'][l]
    lw['conv_b'] = w['conv_b'][l]
    lw['dt_bias_pad'] = jnp.pad(w['dt_bias'][l], (0, LANES - M_HEADS))[None, :]
    lw['a_log_pad'] = jnp.pad(w['a_log'][l], (0, LANES - M_HEADS))[None, :]
    lw['d_skip_t'] = jnp.broadcast_to(jnp.repeat(w['d_skip'][l], M_HEADDIM)[:, None], (M_INNER, LANES))
    lw['m_norm_w'] = w['m_norm_w'][l]
    lw['r_mu'] = w['r_mu'][l]
    lw['r_w0'] = w['r_w0'][l]
    lw['r_a0'] = w['r_a0'][l]
    zeros64 = jnp.zeros((R_LORA_WA // 2, D), F32)
    lw['r_w2_pad'] = jnp.concatenate([w['r_w2'][l], zeros64], axis=0).astype(BF16)
    lw['r_a2_pad'] = jnp.concatenate([zeros64, w['r_a2'][l]], axis=0).astype(BF16)
    lw['r_g2_bf'] = w['r_g2'][l].astype(BF16)
    lw['r_kk'] = w['r_kk'][l]
    lw['r_ka'] = w['r_ka'][l]
    lw['lnx_w_c'] = _chain_const(w['r_lnx_w'][l])
    lw['lnx_b_c'] = _chain_const(w['r_lnx_b'][l])
    lw['rk_c'] = _chain_const(w['r_rk'][l].reshape(-1))
    lw['w_mem_kv_bf'] = w['w_mem_kv'][l].astype(BF16)
    lw['w_mo_bf'] = w['w_mo'][l].astype(BF16)
    lw['w_ro_bf'] = w['w_ro'][l].astype(BF16)
    lw['w_xo_bf'] = w['w_xo'][l].astype(BF16)
    lw['w_o_bf'] = w['w_o'][l].astype(BF16)
    lw['ln1_g'] = w['ln1_g'][l]
    lw['ln1_b'] = w['ln1_b'][l]
    lw['ln2_g'] = w['ln2_g'][l]
    lw['ln2_b'] = w['ln2_b'][l]
    nr = N_EGROUPS + N_EXPERTS
    lw['w_router'] = jnp.pad(jnp.concatenate([w['w_rg'][l], w['w_re'][l]], axis=1),
                             ((0, 0), (0, LANES - nr))).astype(BF16)
    lw['b_router'] = jnp.pad(jnp.concatenate([w['b_rg'][l], w['b_re'][l]]), (0, LANES - nr))[None, :]
    return lw


def kernel(x_prompt, x_sample, mem_prompt, state_ssm, state_conv, state_wkv, state_shift, cache_mem_k, cache_mem_v, w_in, conv_w, conv_b, dt_bias, a_log, d_skip, m_norm_w, r_mu, r_w0, r_w2, r_a0, r_a2, r_g2, r_kk, r_ka, r_rk, r_lnx_w, r_lnx_b, w_mem_kv, w_mo, w_ro, w_xo, w_o, ln1_g, ln1_b, w_rg, b_rg, w_re, b_re, w_gate, w_up, w_down, ln2_g, ln2_b):
    w = dict(w_in=w_in, conv_w=conv_w, conv_b=conv_b, dt_bias=dt_bias, a_log=a_log, d_skip=d_skip,
             m_norm_w=m_norm_w, r_mu=r_mu, r_w0=r_w0, r_w2=r_w2, r_a0=r_a0, r_a2=r_a2, r_g2=r_g2, r_kk=r_kk,
             r_ka=r_ka, r_rk=r_rk, r_lnx_w=r_lnx_w, r_lnx_b=r_lnx_b, w_mem_kv=w_mem_kv, w_mo=w_mo, w_ro=w_ro,
             w_xo=w_xo, w_o=w_o, ln1_g=ln1_g, ln1_b=ln1_b, w_rg=w_rg, b_rg=b_rg, w_re=w_re, b_re=b_re,
             w_gate=w_gate, w_up=w_up, w_down=w_down, ln2_g=ln2_g, ln2_b=ln2_b)
    bp, sp, _ = x_prompt.shape
    bs, ss, _ = x_sample.shape
    n_mem = mem_prompt.shape[1]
    n_p = bp * sp
    x = jnp.concatenate([x_prompt.reshape(n_p, D), x_sample.reshape(bs * ss, D)], axis=0)
    x_bf = x.astype(BF16)
    mem_bf = mem_prompt.reshape(bp * n_mem, D).astype(BF16)
    w_experts = (w_gate, w_up, w_down)
    outs = {n: [] for n in ('p_ssm', 'p_conv', 'p_wkv', 'p_shift', 'p_mk', 'p_mv', 's_ssm', 's_conv', 's_wkv',
                            's_shift')}
    for l in range(DEPTH):
        lw = _layer_weights(l, w)
        mkv = _proj(mem_bf, lw['w_mem_kv_bf'], 512, 2048)
        p = _in_proj(x_bf, lw)
        xp1, lgp, cp, hp, shp, wp = _trunk_mixers(x, p, lw, l, mkv, None, None, None, None, bp, sp, 0)
        xs1, lgs, cs, hs, shs, ws = _trunk_mixers(x, p, lw, l, (cache_mem_k, cache_mem_v), state_conv[l],
                                                  state_ssm, state_shift[l], state_wkv[l], bs, ss, n_p)
        x, x_bf = _hier_moe_ln(jnp.concatenate([xp1, xs1], axis=0), jnp.concatenate([lgp, lgs], axis=0), lw,
                               w_experts, l)
        outs['p_ssm'].append(hp)
        outs['p_conv'].append(cp)
        outs['p_wkv'].append(wp)
        outs['p_shift'].append(shp)
        outs['p_mk'].append(mkv[:, :D].reshape(bp, n_mem, X_HEADS, X_HEADDIM))
        outs['p_mv'].append(mkv[:, D:].reshape(bp, n_mem, X_HEADS, X_HEADDIM))
        outs['s_ssm'].append(hs)
        outs['s_conv'].append(cs)
        outs['s_wkv'].append(ws)
        outs['s_shift'].append(shs)
    st = lambda n: jnp.stack(outs[n])
    return (x[:n_p].reshape(bp, sp, D), x[n_p:].reshape(bs, ss, D), st('p_ssm'), st('p_conv'), st('p_wkv'), st('p_shift'),
            st('p_mk'), st('p_mv'), st('s_ssm'), st('s_conv'), st('s_wkv'), st('s_shift'))
```

```python
import functools

import jax
import jax.numpy as jnp
from jax import lax
from jax.experimental import pallas as pl
from jax.experimental.pallas import tpu as pltpu

F32 = jnp.float32
BF16 = jnp.bfloat16
HIGHEST = lax.Precision.HIGHEST

LANES = 128
SUBLANES = 8
VMEM_LIMIT = 56 * 1024 * 1024

D = 1024
DEPTH = 4
M_INNER = 2048
M_HEADDIM = 64
M_HEADS = 32
M_GROUPS = 8
M_HPG = M_HEADS // M_GROUPS
M_STATE = 128
M_CONV = 4
M_CONV_DIM = M_INNER + 2 * M_GROUPS * M_STATE
M_CHUNK = 128
RMS_EPS = 1e-5
R_HEADS = 16
R_HS = 64
R_LORA_WA = 128
R_LORA_G = 128
R_SHIFT_W = 3 * D + R_LORA_WA + R_LORA_G
R_GN_EPS = 64e-5
MEM_LEN = 256
X_HEADS = 4
X_HEADDIM = 256
OFF_XBC = M_INNER
OFF_DT = OFF_XBC + M_CONV_DIM
OFF_RWKV = OFF_DT + M_HEADS
OFF_Q = OFF_RWKV + R_SHIFT_W
OFF_GATE = OFF_Q + D
N_EGROUPS = 4
E_PER_GROUP = 8
N_EXPERTS = 32
D_EXPERT = 512
LN_EPS = 1e-5
ALPHA = (2 * DEPTH) ** 0.25


def _params(*sem):
    return pltpu.CompilerParams(dimension_semantics=sem, vmem_limit_bytes=VMEM_LIMIT)


def _sigmoid(x):
    return 1.0 / (1.0 + jnp.exp(-x))


def _silu(x):
    return x * _sigmoid(x)


def _softplus(x):
    return jnp.maximum(x, 0.0) + jnp.log1p(jnp.exp(-jnp.abs(x)))


def _layer_norm(h, g, b):
    mu = jnp.mean(h, axis=-1, keepdims=True)
    d = h - mu
    var = jnp.mean(d * d, axis=-1, keepdims=True)
    return d * lax.rsqrt(var + LN_EPS) * g + b


def _proj_kernel(x_ref, w_ref, o_ref):
    o_ref[...] = jnp.dot(x_ref[...], w_ref[...], preferred_element_type=F32)


def _proj(x, w, tm, tn):
    m, k = x.shape
    n = w.shape[1]
    tm = min(tm, m)
    tn = min(tn, n)
    return pl.pallas_call(
        _proj_kernel,
        grid=(n // tn, m // tm),
        in_specs=[pl.BlockSpec((tm, k), lambda j, i: (i, 0)),
                  pl.BlockSpec((k, tn), lambda j, i: (0, j))],
        out_specs=pl.BlockSpec((tm, tn), lambda j, i: (i, j)),
        out_shape=jax.ShapeDtypeStruct((m, n), F32),
        compiler_params=_params("arbitrary", "arbitrary"),
        name="proj",
    )(x, w)


def _mamba_kernel(z_ref, xbc_ref, dt_ref, cbuf_ref, h0_ref, convw_ref, convb_ref, dtb_ref, alog_ref,
                  dskip_ref, mnw_ref, *rest, rows, nc, q, zero_rest):
    y_ref, hT_ref, ubuf, h_scr, yT_scr = rest[-5:]
    c = pl.program_id(1)

    @pl.when(c == 0)
    def _():
        if rows < q:
            ubuf[...] = jnp.zeros(ubuf.shape, F32)
        ubuf[0:SUBLANES, :] = cbuf_ref[0]
        h_scr[...] = h0_ref[0, 0]

    ubuf[SUBLANES:SUBLANES + rows, :] = xbc_ref[...]
    cw = convw_ref[...]
    acc = ubuf[5:5 + q, :] * cw[0:1]
    acc = acc + ubuf[6:6 + q, :] * cw[1:2]
    acc = acc + ubuf[7:7 + q, :] * cw[2:3]
    acc = acc + ubuf[8:8 + q, :] * cw[3:4]
    acc = acc + convb_ref[...]
    if nc > 1:
        ubuf[0:SUBLANES, :] = ubuf[q:q + SUBLANES, :]
    act = _silu(acc)

    row_id = lax.broadcasted_iota(jnp.int32, (q, q), 0)
    col_id = lax.broadcasted_iota(jnp.int32, (q, q), 1)
    dt_raw = dt_ref[...]
    if rows < q:
        dt_raw = jnp.concatenate([dt_raw, jnp.zeros((q - rows, LANES), F32)], axis=0)
    dt = _softplus(dt_raw + dtb_ref[...])
    a_neg = -jnp.exp(alog_ref[...])
    adt = a_neg * dt
    if rows < q:
        valid = lax.broadcasted_iota(jnp.int32, (q, 1), 0) < rows
        act = jnp.where(valid, act, 0.0)
        adt = jnp.where(valid, adt, 0.0)
    tril = (row_id >= col_id).astype(F32)
    acs = jnp.dot(tril, adt, precision=HIGHEST, preferred_element_type=F32)
    acs_t = acs.T
    dt_t = dt.T
    xs_t = act[:, :M_INNER].T
    upper = row_id <= col_id

    for g in range(M_GROUPS):
        b_g = act[:, M_INNER + g * M_STATE:M_INNER + (g + 1) * M_STATE]
        c_g = act[:, M_INNER + (M_GROUPS + g) * M_STATE:M_INNER + (M_GROUPS + g + 1) * M_STATE]
        b_bf = b_g.astype(BF16)
        c_t = c_g.T
        c_t_bf = c_t.astype(BF16)
        cb_t = jnp.dot(b_bf, c_t_bf, preferred_element_type=F32)
        for r in range(M_HPG):
            h = g * M_HPG + r
            a_row = acs_t[h:h + 1, :]
            a_col = acs[:, h:h + 1]
            lm_t = jnp.exp(jnp.where(upper, a_row - a_col, -jnp.inf))
            m_t = (cb_t * lm_t).astype(BF16)
            cs_t = (c_t * jnp.exp(a_row)).astype(BF16)
            x_t = xs_t[h * M_HEADDIM:(h + 1) * M_HEADDIM, :]
            xdt_t = x_t * dt_t[h:h + 1, :]
            hprev = h_scr[h]
            y_t = (jnp.dot(xdt_t.astype(BF16), m_t, preferred_element_type=F32)
                   + jnp.dot(hprev.astype(BF16), cs_t, preferred_element_type=F32))
            yT_scr[h * M_HEADDIM:(h + 1) * M_HEADDIM, :] = y_t + x_t * dskip_ref[h * M_HEADDIM:(h + 1) * M_HEADDIM, 0:q]
            a_last = acs[q - 1:q, h:h + 1]
            wrow = jnp.exp(a_last - a_row)
            s_c = jnp.dot((xdt_t * wrow).astype(BF16), b_bf, preferred_element_type=F32)
            h_scr[h] = hprev * jnp.exp(a_last) + s_c

    y = yT_scr[...].T
    if rows < q:
        y = y[0:rows, :]
    y = y * _silu(z_ref[...])
    gw = M_INNER // M_GROUPS
    parts = []
    for g in range(M_GROUPS):
        yg = y[:, g * gw:(g + 1) * gw]
        ms = jnp.mean(yg * yg, axis=-1, keepdims=True)
        parts.append(yg * lax.rsqrt(ms + RMS_EPS))
    y_ref[...] = jnp.concatenate(parts, axis=-1) * mnw_ref[...]

    @pl.when(c == nc - 1)
    def _():
        hT_ref[0, 0] = h_scr[...]
        if zero_rest:
            hT_ref[1:, 0] = jnp.zeros((hT_ref.shape[0] - 1,) + h_scr.shape, F32)


def _mamba(p_z, p_xbc, p_dt, conv_buf, h0_all, layer, lw, bsz, seq, stack=None):
    if seq >= M_CHUNK:
        q = rows = M_CHUNK
        nc = seq // q
    else:
        q = rows = seq
        nc = 1
    if conv_buf is None:
        cbuf = jnp.zeros((1, SUBLANES, M_CONV_DIM), F32)
        cbuf_map = lambda b, c: (0, 0, 0)
    else:
        cbuf = jnp.pad(conv_buf, ((0, 0), (SUBLANES - (M_CONV - 1), 0), (0, 0)))
        cbuf_map = lambda b, c: (b, 0, 0)
    if h0_all is None:
        h0_all = jnp.zeros((1, 1, M_HEADS, M_HEADDIM, M_STATE), F32)
        h0_map = lambda b, c: (0, 0, 0, 0, 0)
    else:
        h0_map = lambda b, c: (layer, b, 0, 0, 0)
    convw = jnp.pad(lw['conv_w'], ((0, SUBLANES - M_CONV), (0, 0)))
    st_blk = (1, M_HEADS, M_HEADDIM, M_STATE)
    extra_specs, extra_args, aliases = [], (), {}
    if stack is None:
        n_st, st_spec, zero_rest = 1, pl.BlockSpec((1,) + st_blk, lambda b, c: (0, b, 0, 0, 0)), False
    elif isinstance(stack, str):
        n_st, st_spec, zero_rest = DEPTH, pl.BlockSpec((DEPTH,) + st_blk, lambda b, c: (0, b, 0, 0, 0)), True
    else:
        n_st, st_spec, zero_rest = DEPTH, pl.BlockSpec((1,) + st_blk, lambda b, c: (layer, b, 0, 0, 0)), False
        extra_specs, extra_args, aliases = [pl.BlockSpec(memory_space=pl.ANY)], (stack,), {11: 1}
    kern = functools.partial(_mamba_kernel, rows=rows, nc=nc, q=q, zero_rest=zero_rest)
    row_map = lambda b, c: (b * nc + c, 0)
    const2 = lambda b, c: (0, 0)
    y, h_last = pl.pallas_call(
        kern,
        grid=(bsz, nc),
        input_output_aliases=aliases,
        in_specs=[pl.BlockSpec((rows, M_INNER), row_map),
                  pl.BlockSpec((rows, M_CONV_DIM), row_map),
                  pl.BlockSpec((rows, LANES), row_map),
                  pl.BlockSpec((1, SUBLANES, M_CONV_DIM), cbuf_map),
                  pl.BlockSpec((1, 1, M_HEADS, M_HEADDIM, M_STATE), h0_map),
                  pl.BlockSpec((SUBLANES, M_CONV_DIM), const2),
                  pl.BlockSpec((1, M_CONV_DIM), const2),
                  pl.BlockSpec((1, LANES), const2),
                  pl.BlockSpec((1, LANES), const2),
                  pl.BlockSpec((M_INNER, LANES), const2),
                  pl.BlockSpec((1, M_INNER), const2)] + extra_specs,
        out_specs=[pl.BlockSpec((rows, M_INNER), row_map), st_spec],
        out_shape=[jax.ShapeDtypeStruct((bsz * seq, M_INNER), F32),
                   jax.ShapeDtypeStruct((n_st, bsz, M_HEADS, M_HEADDIM, M_STATE), F32)],
        scratch_shapes=[pltpu.VMEM((q + 2 * SUBLANES, M_CONV_DIM), F32),
                        pltpu.VMEM((M_HEADS, M_HEADDIM, M_STATE), F32),
                        pltpu.VMEM((M_INNER, q), F32)],
        compiler_params=_params("arbitrary", "arbitrary"),
        name="mamba_ssd",
    )(p_z, p_xbc, p_dt, cbuf, h0_all, convw, lw['conv_b'][None, :], lw['dt_bias_pad'], lw['a_log_pad'],
      lw['d_skip_t'], lw['m_norm_w'][None, :], *extra_args)
    return y, h_last


def _rwkv_prep_kernel(cols_ref, shift_ref, mu_ref, w0_ref, w2_ref, a0_ref, a2_ref, g2_ref, kkw_ref, kaw_ref,
                      r_out, w_out, k_out, v_out, a_out, kk_out, g_out, sbuf, *, tl, nt):
    t = pl.program_id(1)

    @pl.when(t == 0)
    def _():
        sbuf[0:SUBLANES, :] = shift_ref[0]

    cols = cols_ref[...]
    sbuf[SUBLANES:SUBLANES + tl, :] = cols
    prev = sbuf[SUBLANES - 1:SUBLANES - 1 + tl, :]
    if nt > 1:
        sbuf[0:SUBLANES, :] = sbuf[tl:tl + SUBLANES, :]
    mixed = cols + (prev - cols) * mu_ref[...]
    r = mixed[:, 0:D]
    k = mixed[:, D:2 * D]
    v = mixed[:, 2 * D:3 * D]
    wa = mixed[:, 3 * D:3 * D + R_LORA_WA]
    gl = mixed[:, 3 * D + R_LORA_WA:]
    lw_ = jnp.dot(jnp.tanh(wa).astype(BF16), w2_ref[...], preferred_element_type=F32)
    la_ = jnp.dot(wa.astype(BF16), a2_ref[...], preferred_element_type=F32)
    g = jnp.dot(_sigmoid(gl).astype(BF16), g2_ref[...], preferred_element_type=F32)
    w_log = -_softplus(-(w0_ref[...] + lw_)) - 0.5
    decay = jnp.exp(-jnp.exp(w_log))
    a = _sigmoid(a0_ref[...] + la_)
    r_out[...] = r
    w_out[...] = decay
    k_out[...] = k * (1.0 + (a - 1.0) * kaw_ref[...])
    v_out[...] = v
    a_out[...] = a
    kk_out[...] = k * kkw_ref[...]
    g_out[...] = g


def _rwkv_prep(p_rwkv, shift_buf, lw, bsz, seq):
    tl = min(seq, 256)
    nt = seq // tl
    if shift_buf is None:
        sb = jnp.zeros((1, SUBLANES, R_SHIFT_W), F32)
        sb_map = lambda b, t: (0, 0, 0)
    else:
        sb = jnp.pad(shift_buf[:, None, :], ((0, 0), (SUBLANES - 1, 0), (0, 0)))
        sb_map = lambda b, t: (b, 0, 0)
    row_map = lambda b, t: (b * nt + t, 0)
    const2 = lambda b, t: (0, 0)
    vec = pl.BlockSpec((1, D), const2)
    outs = pl.pallas_call(
        functools.partial(_rwkv_prep_kernel, tl=tl, nt=nt),
        grid=(bsz, nt),
        in_specs=[pl.BlockSpec((tl, R_SHIFT_W), row_map),
                  pl.BlockSpec((1, SUBLANES, R_SHIFT_W), sb_map),
                  pl.BlockSpec((1, R_SHIFT_W), const2),
                  vec, pl.BlockSpec((R_LORA_WA, D), const2),
                  vec, pl.BlockSpec((R_LORA_WA, D), const2),
                  pl.BlockSpec((R_LORA_G, D), const2),
                  vec, vec],
        out_specs=[pl.BlockSpec((tl, D), row_map)] * 7,
        out_shape=[jax.ShapeDtypeStruct((bsz * seq, D), F32)] * 7,
        scratch_shapes=[pltpu.VMEM((tl + 2 * SUBLANES, R_SHIFT_W), F32)],
        compiler_params=_params("arbitrary", "arbitrary"),
        name="rwkv_prep",
    )(p_rwkv, sb, lw['r_mu'][None, :], lw['r_w0'][None, :], lw['r_w2_pad'], lw['r_a0'][None, :], lw['r_a2_pad'],
      lw['r_g2_bf'], lw['r_kk'][None, :], lw['r_ka'][None, :])
    return outs


CH_B = LANES // R_HEADS
CH_HALF = LANES // 2


def _wkv_kernel(r_ref, w_ref, k_ref, v_ref, a_ref, kk_ref, s0_ref, lnw_ref, lnb_ref, rk_ref,
                y_ref, sT_ref, s_scr, ch_scr, m_scr, tok_scr, yflat_scr, *, tb, nb):
    ng = R_HS // SUBLANES
    nhh = R_HEADS // 2
    tblk = pl.program_id(1)
    low_half = lax.broadcasted_iota(jnp.int32, (SUBLANES, LANES), 1) < CH_HALF

    @pl.when(tblk == 0)
    def _():
        s_scr[...] = s0_ref[0]

    sub_id = lax.broadcasted_iota(jnp.int32, (SUBLANES, LANES), 0)

    def swap_rows(vs):
        for s in (1, 2, 4):
            keep_low = (sub_id & s) == 0
            nxt_vs = []
            for i in range(SUBLANES):
                other = vs[i ^ s]
                if i & s:
                    nxt_vs.append(jnp.where(keep_low, pltpu.roll(other, SUBLANES - s, axis=0), vs[i]))
                else:
                    nxt_vs.append(jnp.where(keep_low, vs[i], pltpu.roll(other, s, axis=0)))
            vs = nxt_vs
        return vs

    for idx, ref in enumerate((r_ref, w_ref, k_ref, v_ref, a_ref, kk_ref)):
        def regroup(hh, carry, idx=idx, ref=ref):
            lanes = pl.ds(pl.multiple_of(hh * LANES, LANES), LANES)
            for tg in range(tb // SUBLANES):
                tiles = swap_rows([ref[b, tg * SUBLANES:(tg + 1) * SUBLANES, lanes] for b in range(CH_B)])
                for tt in range(SUBLANES):
                    tok_scr[idx, hh, tg * SUBLANES + tt] = tiles[tt]
            return carry

        lax.fori_loop(0, nhh, regroup, 0)

    def rows_at(idx, t, hh):
        return tok_scr[idx, hh, t]

    def stage_rows(t):
        for p in range(3):
            for hh in range(nhh):
                av, bv = rows_at(2 * p, t, hh), rows_at(2 * p + 1, t, hh)
                m_scr[p, hh * CH_B:(hh + 1) * CH_B, :] = jnp.where(low_half, av, pltpu.roll(bv, CH_HALF, axis=1))
                m_scr[p, CH_HALF + hh * CH_B:CH_HALF + (hh + 1) * CH_B, :] = jnp.where(
                    low_half, pltpu.roll(av, CH_HALF, axis=1), bv)

    def stage_chain(slot):
        rw, kv, akk = m_scr[0].T, m_scr[1].T, m_scr[2].T
        kkr = akk[R_HS:]
        nrm = jnp.sqrt(jnp.sum(kkr * kkr, axis=0, keepdims=True))
        kk = kkr / jnp.maximum(nrm, 1e-12)
        for idx, val in enumerate((rw[:R_HS], rw[R_HS:], kv[:R_HS], kv[R_HS:], kk * akk[:R_HS], -kk)):
            ch_scr[slot, idx] = val

    def row(ref_view, j):
        return jnp.broadcast_to(ref_view[pl.ds(j, 1), :], (SUBLANES, LANES))

    zeros = tuple(jnp.zeros((SUBLANES, LANES), F32) for _ in range(ng))
    stage_rows(0)
    stage_chain(0)
    stage_rows(min(1, tb - 1))
    stage_chain(1)
    stage_rows(min(2, tb - 1))

    def first_sa(j, acc):
        aj = row(ch_scr.at[0, 5], j)
        return tuple(acc[ig] + s_scr[ig, j] * aj for ig in range(ng))

    sa0 = lax.fori_loop(0, R_HS, first_sa, zeros)

    def step(t, sa):
        cur = t % 3
        nxt = (t + 1) % 3
        v = ch_scr[cur, 3]
        vs = tuple(v[ig * SUBLANES:(ig + 1) * SUBLANES, :] for ig in range(ng))
        r_t, w_t, k_t, b_t, a_next = (ch_scr.at[cur, 0], ch_scr.at[cur, 1], ch_scr.at[cur, 2], ch_scr.at[cur, 4],
                                      ch_scr.at[nxt, 5])

        def col(j, carry):
            yacc, san = carry
            wj, bj, kj, rj, aj = row(w_t, j), row(b_t, j), row(k_t, j), row(r_t, j), row(a_next, j)
            ynew, snew = [], []
            for ig in range(ng):
                s = s_scr[ig, j] * wj + sa[ig] * bj + vs[ig] * kj
                s_scr[ig, j] = s
                ynew.append(yacc[ig] + s * rj)
                snew.append(san[ig] + s * aj)
            return tuple(ynew), tuple(snew)

        yacc, san = lax.fori_loop(0, R_HS, col, (zeros, zeros), unroll=8)
        y = jnp.concatenate(yacc, axis=0)
        mu = jnp.mean(y, axis=0, keepdims=True)
        d = y - mu
        var = jnp.mean(d * d, axis=0, keepdims=True)
        yn = d * lax.rsqrt(var + R_GN_EPS) * lnw_ref[...] + lnb_ref[...]
        bonus = jnp.sum(ch_scr[cur, 0] * ch_scr[cur, 2] * rk_ref[...], axis=0, keepdims=True) * v
        yo = yn + bonus
        yt = jnp.concatenate([yo, yo], axis=0).T
        for hh in range(nhh):
            even = yt[hh * CH_B:(hh + 1) * CH_B]
            odd = yt[CH_HALF + hh * CH_B:CH_HALF + (hh + 1) * CH_B]
            yflat_scr[hh, t] = jnp.where(low_half, even, odd)
        stage_chain((t + 2) % 3)
        stage_rows(jnp.minimum(t + 3, tb - 1))
        return tuple(san)

    lax.fori_loop(0, tb, step, sa0)

    def ungroup(hh, carry):
        lanes = pl.ds(pl.multiple_of(hh * LANES, LANES), LANES)
        for tg in range(tb // SUBLANES):
            tiles = swap_rows([yflat_scr[hh, tg * SUBLANES + tt] for tt in range(SUBLANES)])
            for b in range(CH_B):
                y_ref[b, tg * SUBLANES:(tg + 1) * SUBLANES, lanes] = tiles[b]
        return carry

    lax.fori_loop(0, nhh, ungroup, 0)

    @pl.when(tblk == nb - 1)
    def _():
        sT_ref[0] = s_scr[...]


def _chain_const(t):
    th = t.reshape(R_HEADS // 2, 2, R_HS).transpose(2, 1, 0)
    return jnp.broadcast_to(th[..., None], (R_HS, 2, R_HEADS // 2, CH_B)).reshape(R_HS, LANES)


def _wkv(r, w, k, v, a, kk, s0, lw, bsz, seq):
    g = bsz // CH_B
    ng = R_HS // SUBLANES
    nhh = R_HEADS // 2
    tb = min(seq, 32)
    nb = seq // tb
    tok =[t.reshape(bsz, seq, D) for t in (r, w, k, v, a, kk)]
    st_shape = (ng, R_HS, SUBLANES, LANES)
    if s0 is None:
        s0c = jnp.zeros((1,) + st_shape, F32)
        s_in = pl.BlockSpec((1,) + st_shape, lambda gi, ti: (0, 0, 0, 0, 0))
    else:
        s0c = s0.reshape(g, CH_B, nhh, 2, ng, SUBLANES, R_HS).transpose(0, 4, 6, 5, 3, 2, 1).reshape((g,) + st_shape)
        s_in = pl.BlockSpec((1,) + st_shape, lambda gi, ti: (gi, 0, 0, 0, 0))
    blk = pl.BlockSpec((CH_B, tb, D), lambda gi, ti: (gi, ti, 0))
    s_out_spec = pl.BlockSpec((1,) + st_shape, lambda gi, ti: (gi, 0, 0, 0, 0))
    cblk = pl.BlockSpec((R_HS, LANES), lambda gi, ti: (0, 0))
    y, s_last = pl.pallas_call(
        functools.partial(_wkv_kernel, tb=tb, nb=nb),
        grid=(g, nb),
        in_specs=[blk] * 6 + [s_in, cblk, cblk, cblk],
        out_specs=[blk, s_out_spec],
        out_shape=[jax.ShapeDtypeStruct((bsz, seq, D), F32),
                   jax.ShapeDtypeStruct((g,) + st_shape, F32)],
        scratch_shapes=[pltpu.VMEM(st_shape, F32),
                        pltpu.VMEM((3, 6, R_HS, LANES), F32),
                        pltpu.VMEM((3, LANES, LANES), F32),
                        pltpu.VMEM((6, nhh, tb, CH_B, LANES), F32),
                        pltpu.VMEM((nhh, tb, CH_B, LANES), F32)],
        compiler_params=_params("arbitrary", "arbitrary"),
        name="wkv7",
    )(*tok, s0c, lw['lnx_w_c'], lw['lnx_b_c'], lw['rk_c'])
    s_out = s_last.reshape(g, ng, R_HS, SUBLANES, 2, nhh, CH_B).transpose(0, 6, 5, 4, 1, 3, 2)
    return y.reshape(bsz * seq, D), s_out.reshape(bsz, R_HEADS, R_HS, R_HS)


def _attn_kernel(q_ref, k_ref, v_ref, o_ref, *, head_major_cols):
    q = q_ref[...]
    scale = X_HEADDIM ** -0.5
    outs = []
    if not head_major_cols:
        tl = q.shape[0]
        n_rows = MEM_LEN * X_HEADS
        k_all = k_ref[0, 0].reshape(n_rows, X_HEADDIM).astype(BF16)
        v_all = v_ref[0, 0].reshape(n_rows, X_HEADDIM).astype(BF16)
        q_all = jnp.concatenate([q[:, h * X_HEADDIM:(h + 1) * X_HEADDIM] for h in range(X_HEADS)], axis=0)
        s = lax.dot_general(q_all.astype(BF16), k_all, (((1,), (1,)), ((), ())), preferred_element_type=F32) * scale
        q_head = lax.broadcasted_iota(jnp.int32, (X_HEADS * tl, n_rows), 0) // tl
        kv_head = lax.broadcasted_iota(jnp.int32, (X_HEADS * tl, n_rows), 1) % X_HEADS
        s = jnp.where(q_head == kv_head, s, -jnp.inf)
        s = s - jnp.max(s, axis=-1, keepdims=True)
        e = jnp.exp(s)
        p = e / jnp.sum(e, axis=-1, keepdims=True)
        o_all = jnp.dot(p.astype(BF16), v_all, preferred_element_type=F32)
        o_ref[...] = jnp.concatenate([o_all[h * tl:(h + 1) * tl] for h in range(X_HEADS)], axis=-1)
        return
    for h in range(X_HEADS):
        sl = slice(h * X_HEADDIM, (h + 1) * X_HEADDIM)
        qh = q[:, sl].astype(BF16)
        kh = k_ref[:, sl].astype(BF16)
        vh = v_ref[:, sl].astype(BF16)
        s = lax.dot_general(qh, kh, (((1,), (1,)), ((), ())), preferred_element_type=F32) * scale
        s = s - jnp.max(s, axis=-1, keepdims=True)
        e = jnp.exp(s)
        p = e / jnp.sum(e, axis=-1, keepdims=True)
        outs.append(jnp.dot(p.astype(BF16), vh, preferred_element_type=F32))
    o_ref[...] = jnp.concatenate(outs, axis=-1)


def _attend(p_q, mem, layer, bsz, seq):
    tl = min(seq, 512)
    nt = seq // tl
    if isinstance(mem, tuple):
        cache_k, cache_v = mem
        cblk = pl.BlockSpec((1, 1, MEM_LEN, X_HEADS, X_HEADDIM), lambda b, t: (layer, b, 0, 0, 0))
        kv_specs, kv_args, head_major_cols = [cblk, cblk], (cache_k, cache_v), False
    else:
        kv_specs = [pl.BlockSpec((MEM_LEN, D), lambda b, t: (b, 0)), pl.BlockSpec((MEM_LEN, D), lambda b, t: (b, 1))]
        kv_args, head_major_cols = (mem, mem), True
    return pl.pallas_call(
        functools.partial(_attn_kernel, head_major_cols=head_major_cols),
        grid=(bsz, nt),
        in_specs=[pl.BlockSpec((tl, D), lambda b, t: (b * nt + t, 0))] + kv_specs,
        out_specs=pl.BlockSpec((tl, D), lambda b, t: (b * nt + t, 0)),
        out_shape=jax.ShapeDtypeStruct((bsz * seq, D), F32),
        compiler_params=_params("arbitrary", "arbitrary"),
        name="mem_attn",
    )(p_q, *kv_args)


def _merge_kernel(x_ref, ya_ref, yb_ref, g_ref, yc_ref, gate_ref, wmo_ref, wro_ref, wxo_ref, wo_ref,
                  lng_ref, lnb_ref, wr_ref, br_ref, x1_ref, lg_ref):
    gate = gate_ref[...]
    ma = jnp.dot(ya_ref[...].astype(BF16), wmo_ref[...], preferred_element_type=F32)
    mb = jnp.dot((yb_ref[...] * g_ref[...]).astype(BF16), wro_ref[...], preferred_element_type=F32)
    mc = jnp.dot(yc_ref[...].astype(BF16), wxo_ref[...], preferred_element_type=F32)
    merged = (_sigmoid(gate[:, 0:D]) * ma + _sigmoid(gate[:, D:2 * D]) * mb) + _sigmoid(gate[:, 2 * D:3 * D]) * mc
    h = ALPHA * x_ref[...] + jnp.dot(merged.astype(BF16), wo_ref[...], preferred_element_type=F32)
    x1 = _layer_norm(h, lng_ref[...], lnb_ref[...])
    x1_ref[...] = x1
    lg_ref[...] = jnp.dot(x1.astype(BF16), wr_ref[...], preferred_element_type=F32) + br_ref[...]


def _merge(x, y_a, y_b, g_b, y_c, p_gate, lw):
    m = x.shape[0]
    tm = min(m, 256)
    row = lambda w: pl.BlockSpec((tm, w), lambda i: (i, 0))
    full = lambda a, b: pl.BlockSpec((a, b), lambda i: (0, 0))
    return pl.pallas_call(
        _merge_kernel,
        grid=(m // tm,),
        in_specs=[row(D), row(M_INNER), row(D), row(D), row(D), row(3 * D),
                  full(M_INNER, D), full(D, D), full(D, D), full(D, D),
                  full(1, D), full(1, D), full(D, LANES), full(1, LANES)],
        out_specs=[row(D), row(LANES)],
        out_shape=[jax.ShapeDtypeStruct((m, D), F32), jax.ShapeDtypeStruct((m, LANES), F32)],
        compiler_params=_params("arbitrary"),
        name="merge_ln_router",
    )(x, y_a, y_b, g_b, y_c, p_gate, lw['w_mo_bf'], lw['w_ro_bf'], lw['w_xo_bf'], lw['w_o_bf'],
      lw['ln1_g'][None, :], lw['ln1_b'][None, :], lw['w_router'], lw['b_router'])


def _moe_kernel(be_ref, nu_ref, x_ref, wg_ref, wu_ref, wd_ref, o_ref, wg_bf, wu_bf, wd_bf):
    i = pl.program_id(0)
    active = i < nu_ref[0]
    new_expert = jnp.logical_or(i == 0, be_ref[i] != be_ref[jnp.maximum(i - 1, 0)])

    @pl.when(jnp.logical_and(active, new_expert))
    def _():
        wg_bf[...] = wg_ref[0, 0].astype(BF16)
        wu_bf[...] = wu_ref[0, 0].astype(BF16)
        wd_bf[...] = wd_ref[0, 0].astype(BF16)

    @pl.when(active)
    def _():
        xb = x_ref[...].astype(BF16)
        hid = _silu(jnp.dot(xb, wg_bf[...], preferred_element_type=F32)) * jnp.dot(xb, wu_bf[...],
                                                                                    preferred_element_type=F32)
        o_ref[...] = jnp.dot(hid.astype(BF16), wd_bf[...], preferred_element_type=F32)

    @pl.when(jnp.logical_not(active))
    def _():
        o_ref[...] = jnp.zeros(o_ref.shape, F32)


def _moe_experts(xb, blk_exp, n_used, w_gate, w_up, w_down, layer, blk):
    n_blk = xb.shape[0] // blk
    grid_spec = pltpu.PrefetchScalarGridSpec(
        num_scalar_prefetch=2,
        grid=(n_blk,),
        in_specs=[pl.BlockSpec((blk, D), lambda i, be, nu: (i, 0)),
                  pl.BlockSpec((1, 1, D, D_EXPERT), lambda i, be, nu: (layer, be[i], 0, 0)),
                  pl.BlockSpec((1, 1, D, D_EXPERT), lambda i, be, nu: (layer, be[i], 0, 0)),
                  pl.BlockSpec((1, 1, D_EXPERT, D), lambda i, be, nu: (layer, be[i], 0, 0))],
        out_specs=pl.BlockSpec((blk, D), lambda i, be, nu: (i, 0)),
        scratch_shapes=[pltpu.VMEM((D, D_EXPERT), BF16), pltpu.VMEM((D, D_EXPERT), BF16),
                        pltpu.VMEM((D_EXPERT, D), BF16)],
    )
    return pl.pallas_call(
        _moe_kernel,
        grid_spec=grid_spec,
        out_shape=jax.ShapeDtypeStruct((n_blk * blk, D), F32),
        compiler_params=_params("arbitrary"),
        name="moe_experts",
    )(blk_exp, n_used, xb, w_gate, w_up, w_down)


def _combine_kernel(x_ref, y0_ref, y1_ref, gt_ref, lng_ref, lnb_ref, o_ref, ob_ref):
    gt = gt_ref[...]
    moe = gt[:, 0:1] * y0_ref[...] + gt[:, 1:2] * y1_ref[...]
    x2 = _layer_norm(ALPHA * x_ref[...] + moe, lng_ref[...], lnb_ref[...])
    o_ref[...] = x2
    ob_ref[...] = x2.astype(BF16)


def _combine(x1, y0, y1, gates, lw):
    m = x1.shape[0]
    tm = min(m, 512)
    row = lambda w: pl.BlockSpec((tm, w), lambda i: (i, 0))
    full = lambda a, b: pl.BlockSpec((a, b), lambda i: (0, 0))
    return pl.pallas_call(
        _combine_kernel,
        grid=(m // tm,),
        in_specs=[row(D), row(D), row(D), row(LANES), full(1, D), full(1, D)],
        out_specs=[row(D), row(D)],
        out_shape=[jax.ShapeDtypeStruct((m, D), F32), jax.ShapeDtypeStruct((m, D), BF16)],
        compiler_params=_params("arbitrary"),
        name="moe_combine_ln",
    )(x1, y0, y1, gates, lw['ln2_g'][None, :], lw['ln2_b'][None, :])


def _hier_moe_ln(x1, logits, lw, w_experts, layer):
    t = x1.shape[0]
    blk = 256 if t >= 8192 else 128
    lg = logits[:, :N_EGROUPS]
    le = logits[:, N_EGROUPS:N_EGROUPS + N_EXPERTS].reshape(t, N_EGROUPS, E_PER_GROUP)
    g_sel = jnp.argmax(lg, axis=-1).astype(jnp.int32)
    g_prob = jnp.take_along_axis(jax.nn.softmax(lg, axis=-1), g_sel[:, None], axis=-1)
    le = jnp.take_along_axis(le, g_sel[:, None, None], axis=1)[:, 0]
    top_v, top_i = lax.top_k(le, 2)
    gate = g_prob * jax.nn.softmax(top_v, axis=-1)
    flat_e = (g_sel[:, None] * E_PER_GROUP + top_i.astype(jnp.int32)).reshape(-1)
    n_assign = 2 * t
    order = jnp.argsort(flat_e).astype(jnp.int32)
    onehot = (flat_e[:, None] == jnp.arange(N_EXPERTS, dtype=jnp.int32)[None, :]).astype(jnp.int32)
    seen = jnp.cumsum(onehot, axis=0)
    counts = seen[-1]
    rank = jnp.sum(seen * onehot, axis=1) - 1
    padded = (counts + blk - 1) // blk * blk
    pend = jnp.cumsum(padded)
    pstart = pend - padded
    cstart = jnp.cumsum(counts) - counts
    dest = pstart[flat_e] + rank
    n_blk = n_assign // blk + N_EXPERTS
    blk_start = jnp.arange(n_blk, dtype=jnp.int32) * blk
    blk_exp = jnp.minimum(jnp.sum(pend[None, :] <= blk_start[:, None], axis=1), N_EXPERTS - 1).astype(jnp.int32)
    slot = jnp.arange(n_blk * blk, dtype=jnp.int32)
    slot_e = jnp.repeat(blk_exp, blk)
    off = slot - pstart[slot_e]
    src = order[jnp.clip(cstart[slot_e] + off, 0, n_assign - 1)] // 2
    slot_ok = (off < counts[slot_e]) & (slot < pend[-1])
    slot_tok = jnp.where(slot_ok, src, 0)
    xb = x1[slot_tok]
    n_used = (pend[-1] // blk).astype(jnp.int32)[None]
    yb = _moe_experts(xb, blk_exp, n_used, *w_experts, layer, blk)
    dest2 = dest.reshape(t, 2)
    gates = jnp.pad(gate, ((0, 0), (0, LANES - 2)))
    return _combine(x1, yb[dest2[:, 0]], yb[dest2[:, 1]], gates, lw)


def _trunk_mixers(x, x_bf, lw, layer, mem, conv_buf, ssm_all, shift_buf, wkv_s, bsz, seq, ssm_stack=None):
    tm = 512
    p_z = _proj(x_bf, lw['w_in_z'], tm, 2048)
    p_xbc = _proj(x_bf, lw['w_in_xbc'], tm, 2048)
    p_dt = _proj(x_bf, lw['w_in_dt'], tm, LANES)
    p_rwkv = _proj(x_bf, lw['w_in_rwkv'], tm, R_SHIFT_W // 2)
    p_q = _proj(x_bf, lw['w_in_q'], tm, 1024)
    p_gate = _proj(x_bf, lw['w_in_gate'], tm, 1536)

    y_a, ssm_new = _mamba(p_z, p_xbc, p_dt, conv_buf, ssm_all, layer, lw, bsz, seq, ssm_stack)
    conv_new = p_xbc.reshape(bsz, seq, M_CONV_DIM)[:, seq - (M_CONV - 1):]
    r, w, k, v, a, kk, g_b = _rwkv_prep(p_rwkv, shift_buf, lw, bsz, seq)
    y_b, wkv_new = _wkv(r, w, k, v, a, kk, wkv_s, lw, bsz, seq)
    shift_new = p_rwkv.reshape(bsz, seq, R_SHIFT_W)[:, -1]
    y_c = _attend(p_q, mem, layer, bsz, seq)
    x1, logits = _merge(x, y_a, y_b, g_b, y_c, p_gate, lw)
    return x1, logits, conv_new, ssm_new, shift_new, wkv_new


def _layer_weights(l, w):
    w_in = w['w_in'][l]
    lw = {}
    lw['w_in_z'] = w_in[:, :OFF_XBC].astype(BF16)
    lw['w_in_xbc'] = w_in[:, OFF_XBC:OFF_DT].astype(BF16)
    lw['w_in_dt'] = jnp.pad(w_in[:, OFF_DT:OFF_RWKV], ((0, 0), (0, LANES - M_HEADS))).astype(BF16)
    lw['w_in_rwkv'] = w_in[:, OFF_RWKV:OFF_Q].astype(BF16)
    lw['w_in_q'] = w_in[:, OFF_Q:OFF_GATE].astype(BF16)
    lw['w_in_gate'] = w_in[:, OFF_GATE:].astype(BF16)
    lw['conv_w'] = w['conv_w'][l]
    lw['conv_b'] = w['conv_b'][l]
    lw['dt_bias_pad'] = jnp.pad(w['dt_bias'][l], (0, LANES - M_HEADS))[None, :]
    lw['a_log_pad'] = jnp.pad(w['a_log'][l], (0, LANES - M_HEADS))[None, :]
    lw['d_skip_t'] = jnp.broadcast_to(jnp.repeat(w['d_skip'][l], M_HEADDIM)[:, None], (M_INNER, LANES))
    lw['m_norm_w'] = w['m_norm_w'][l]
    lw['r_mu'] = w['r_mu'][l]
    lw['r_w0'] = w['r_w0'][l]
    lw['r_a0'] = w['r_a0'][l]
    zeros64 = jnp.zeros((R_LORA_WA // 2, D), F32)
    lw['r_w2_pad'] = jnp.concatenate([w['r_w2'][l], zeros64], axis=0).astype(BF16)
    lw['r_a2_pad'] = jnp.concatenate([zeros64, w['r_a2'][l]], axis=0).astype(BF16)
    lw['r_g2_bf'] = w['r_g2'][l].astype(BF16)
    lw['r_kk'] = w['r_kk'][l]
    lw['r_ka'] = w['r_ka'][l]
    lw['lnx_w_c'] = _chain_const(w['r_lnx_w'][l])
    lw['lnx_b_c'] = _chain_const(w['r_lnx_b'][l])
    lw['rk_c'] = _chain_const(w['r_rk'][l].reshape(-1))
    lw['w_mem_kv_bf'] = w['w_mem_kv'][l].astype(BF16)
    lw['w_mo_bf'] = w['w_mo'][l].astype(BF16)
    lw['w_ro_bf'] = w['w_ro'][l].astype(BF16)
    lw['w_xo_bf'] = w['w_xo'][l].astype(BF16)
    lw['w_o_bf'] = w['w_o'][l].astype(BF16)
    lw['ln1_g'] = w['ln1_g'][l]
    lw['ln1_b'] = w['ln1_b'][l]
    lw['ln2_g'] = w['ln2_g'][l]
    lw['ln2_b'] = w['ln2_b'][l]
    nr = N_EGROUPS + N_EXPERTS
    lw['w_router'] = jnp.pad(jnp.concatenate([w['w_rg'][l], w['w_re'][l]], axis=1),
                             ((0, 0), (0, LANES - nr))).astype(BF16)
    lw['b_router'] = jnp.pad(jnp.concatenate([w['b_rg'][l], w['b_re'][l]]), (0, LANES - nr))[None, :]
    return lw


def kernel(x_prompt, x_sample, mem_prompt, state_ssm, state_conv, state_wkv, state_shift, cache_mem_k, cache_mem_v, w_in, conv_w, conv_b, dt_bias, a_log, d_skip, m_norm_w, r_mu, r_w0, r_w2, r_a0, r_a2, r_g2, r_kk, r_ka, r_rk, r_lnx_w, r_lnx_b, w_mem_kv, w_mo, w_ro, w_xo, w_o, ln1_g, ln1_b, w_rg, b_rg, w_re, b_re, w_gate, w_up, w_down, ln2_g, ln2_b):
    w = dict(w_in=w_in, conv_w=conv_w, conv_b=conv_b, dt_bias=dt_bias, a_log=a_log, d_skip=d_skip,
             m_norm_w=m_norm_w, r_mu=r_mu, r_w0=r_w0, r_w2=r_w2, r_a0=r_a0, r_a2=r_a2, r_g2=r_g2, r_kk=r_kk,
             r_ka=r_ka, r_rk=r_rk, r_lnx_w=r_lnx_w, r_lnx_b=r_lnx_b, w_mem_kv=w_mem_kv, w_mo=w_mo, w_ro=w_ro,
             w_xo=w_xo, w_o=w_o, ln1_g=ln1_g, ln1_b=ln1_b, w_rg=w_rg, b_rg=b_rg, w_re=w_re, b_re=b_re,
             w_gate=w_gate, w_up=w_up, w_down=w_down, ln2_g=ln2_g, ln2_b=ln2_b)
    bp, sp, _ = x_prompt.shape
    bs, ss, _ = x_sample.shape
    n_mem = mem_prompt.shape[1]
    xp = x_prompt.reshape(bp * sp, D)
    xs = x_sample.reshape(bs * ss, D)
    mem_bf = mem_prompt.reshape(bp * n_mem, D).astype(BF16)
    xp_bf = xp.astype(BF16)
    xs_bf = xs.astype(BF16)
    w_experts = (w_gate, w_up, w_down)
    outs = {n: [] for n in ('p_ssm', 'p_conv', 'p_wkv', 'p_shift', 'p_mk', 'p_mv', 's_ssm', 's_conv', 's_wkv',
                            's_shift')}
    for l in range(DEPTH):
        lw = _layer_weights(l, w)
        mkv = _proj(mem_bf, lw['w_mem_kv_bf'], 512, 2048)
        xp1, lgp, cp, hp, shp, wp = _trunk_mixers(xp, xp_bf, lw, l, mkv, None, None, None, None, bp, sp)
        xs1, lgs, cs, hs, shs, ws = _trunk_mixers(xs, xs_bf, lw, l, (cache_mem_k, cache_mem_v), state_conv[l],
                                                  state_ssm, state_shift[l], state_wkv[l], bs, ss,
                                                  'new' if l == 0 else hs)
        xp, xp_bf = _hier_moe_ln(xp1, lgp, lw, w_experts, l)
        xs, xs_bf = _hier_moe_ln(xs1, lgs, lw, w_experts, l)
        outs['p_ssm'].append(hp[0])
        outs['p_conv'].append(cp)
        outs['p_wkv'].append(wp)
        outs['p_shift'].append(shp)
        outs['p_mk'].append(mkv[:, :D].reshape(bp, n_mem, X_HEADS, X_HEADDIM))
        outs['p_mv'].append(mkv[:, D:].reshape(bp, n_mem, X_HEADS, X_HEADDIM))
        outs['s_conv'].append(cs)
        outs['s_wkv'].append(ws)
        outs['s_shift'].append(shs)
    st = lambda n: jnp.stack(outs[n])
    return (xp.reshape(bp, sp, D), xs.reshape(bs, ss, D), st('p_ssm'), st('p_conv'), st('p_wkv'), st('p_shift'),
            st('p_mk'), st('p_mv'), hs, st('s_conv'), st('s_wkv'), st('s_shift'))
```
